```python
import jax
import jax.numpy as jnp
from jax import lax
import numpy as np

D_MODEL = 1024
BATCH = 16
SEQ = 2048
DEPTH = 4

N_META = 16
POOL_WIDTH = D_MODEL // 2
POOL_WINDOWS = (2, 4, 8, 16)
N_POOL_GROUPS = len(POOL_WINDOWS)
POOL_GROUP = POOL_WIDTH // N_POOL_GROUPS
RWKV_WIDTH = D_MODEL - POOL_WIDTH
HEAD_SIZE = 64
N_RWKV_HEADS = RWKV_WIDTH // HEAD_SIZE
DECAY_LORA = 64
AAA_LORA = 64
MV_LORA = 32
GATE_LORA = 160
D_FF = -(-8 * D_MODEL // (3 * 256)) * 256
RMS_EPS = 1e-6
GN_EPS = 64e-5
L2_EPS = 1e-12

SHIFT_WIDTHS = (RWKV_WIDTH, RWKV_WIDTH, RWKV_WIDTH, DECAY_LORA, AAA_LORA, GATE_LORA)
N_SHIFT = sum(SHIFT_WIDTHS)
N_IN = POOL_WIDTH + N_SHIFT

kernel_name = "hymba_pool_rwkv7_hybrid"


def rms_norm(x, g):
    xf = x.astype(jnp.float32)
    y = xf * lax.rsqrt(jnp.mean(xf * xf, axis=-1, keepdims=True) + RMS_EPS)
    return (y * g.astype(jnp.float32)).astype(x.dtype)


def split_cols(p, widths):
    idx = np.cumsum(widths)[:-1].tolist()
    return jnp.split(p, idx, axis=-1)


def token_shift(p, mu):
    prev = jnp.pad(p, ((0, 0), (1, 0), (0, 0)))[:, :-1]
    return p + (prev - p) * mu


def to_heads(z):
    return z.reshape(z.shape[:-1] + (N_RWKV_HEADS, HEAD_SIZE))


def multiscale_pool(u, w_grp, scale):
    T = u.shape[1]
    uf = u.astype(jnp.float32)
    cs = jnp.pad(jnp.cumsum(uf, axis=1), ((0, 0), (1, 0), (0, 0)))
    pos = jnp.arange(T)
    outs = []
    for g, win in enumerate(POOL_WINDOWS):
        c = cs[..., g * POOL_GROUP:(g + 1) * POOL_GROUP]
        upper = c[:, 1:]
        lower = jnp.pad(c[:, :T + 1 - win], ((0, 0), (win - 1, 0), (0, 0)))
        count = jnp.minimum(pos + 1, win).astype(jnp.float32)[None, :, None]
        outs.append((upper - lower) / count)
    pooled = jnp.stack(outs, axis=2)
    diff = pooled - uf.reshape(pooled.shape)
    y = jnp.einsum('btgc,gcd->btgd', diff, w_grp.astype(jnp.float32))
    return (y.reshape(u.shape) * scale.astype(jnp.float32)).astype(u.dtype)


def rwkv7_scan(r, w, k, v, a, b):
    bsz = r.shape[0]

    def step(S, inp):
        r_t, w_t, k_t, v_t, a_t, b_t = inp
        sa = jnp.einsum('bhvk,bhk->bhv', S, a_t)
        S = (S * w_t[:, :, None, :] + sa[..., None] * b_t[:, :, None, :]
             + v_t[..., None] * k_t[:, :, None, :])
        y = jnp.einsum('bhvk,bhk->bhv', S, r_t)
        return S, y

    xs = tuple(jnp.moveaxis(z, 1, 0) for z in (r, w, k, v, a, b))
    S0 = jnp.zeros((bsz, N_RWKV_HEADS, HEAD_SIZE, HEAD_SIZE), jnp.float32)
    _, ys = lax.scan(step, S0, xs)
    return jnp.moveaxis(ys, 0, 1)


def setup_inputs(seed: int = 0) -> dict:
    key = jax.random.key(seed)
    ks = iter(jax.random.split(key, 32))

    def nrm(shape, scale):
        return jax.random.normal(next(ks), shape, jnp.float32) * scale

    def unif(shape, lo, hi):
        return jax.random.uniform(next(ks), shape, jnp.float32, lo, hi)

    L, Lv = DEPTH, DEPTH - 1
    return {
        "x": nrm((BATCH, SEQ, D_MODEL), 1.0),
        "meta_tokens": nrm((N_META, D_MODEL), 1.0),
        "ln1_g": 1.0 + nrm((L, D_MODEL), 0.05),
        "w_in": nrm((L, D_MODEL, N_IN), D_MODEL ** -0.5),
        "mu_shift": unif((L, N_SHIFT), 0.0, 1.0),
        "pool_w": nrm((L, N_POOL_GROUPS, POOL_GROUP, POOL_GROUP), POOL_GROUP ** -0.5),
        "pool_scale": 1.0 + nrm((L, POOL_WIDTH), 0.1),
        "w0": unif((L, RWKV_WIDTH), -6.0, -1.0),
        "w_lora_up": nrm((L, DECAY_LORA, RWKV_WIDTH), 0.5 * DECAY_LORA ** -0.5),
        "a0": nrm((L, RWKV_WIDTH), 0.1),
        "a_lora_up": nrm((L, AAA_LORA, RWKV_WIDTH), AAA_LORA ** -0.5),
        "g_lora_up": nrm((L, GATE_LORA, RWKV_WIDTH), GATE_LORA ** -0.5),
        "k_k": 0.85 + nrm((L, RWKV_WIDTH), 0.05),
        "k_a": 1.0 + nrm((L, RWKV_WIDTH), 0.05),
        "r_k": nrm((L, RWKV_WIDTH), 0.1),
        "gn_w": 1.0 + nrm((L, RWKV_WIDTH), 0.05),
        "gn_b": nrm((L, RWKV_WIDTH), 0.01),
        "w_in_vres": nrm((Lv, D_MODEL, MV_LORA), D_MODEL ** -0.5),
        "mu_vres": unif((Lv, MV_LORA), 0.0, 1.0),
        "v0": nrm((Lv, RWKV_WIDTH), 0.1),
        "v_lora_up": nrm((Lv, MV_LORA, RWKV_WIDTH), MV_LORA ** -0.5),
        "w_out": nrm((L, D_MODEL, D_MODEL), D_MODEL ** -0.5),
        "ln2_g": 1.0 + nrm((L, D_MODEL), 0.05),
        "w_gate_up": nrm((L, D_MODEL, 2 * D_FF), D_MODEL ** -0.5),
        "w_down": nrm((L, D_FF, D_MODEL), D_FF ** -0.5),
        "final_g": 1.0 + nrm((D_MODEL,), 0.05),
    }


def reference(x, meta_tokens, ln1_g, w_in, mu_shift, pool_w, pool_scale, w0, w_lora_up,
              a0, a_lora_up, g_lora_up, k_k, k_a, r_k, gn_w, gn_b, w_in_vres, mu_vres,
              v0, v_lora_up, w_out, ln2_g, w_gate_up, w_down, final_g):
    f32 = jnp.float32
    bsz = x.shape[0]
    dt = x.dtype
    meta = jnp.broadcast_to(meta_tokens.astype(dt)[None], (bsz, N_META, D_MODEL))
    h = jnp.concatenate([meta, x], axis=1)
    T = h.shape[1]
    v_first = None
    for l in range(DEPTH):
        hn = rms_norm(h, ln1_g[l])
        if l == 0:
            w_comb, mu = w_in[0], mu_shift[0]
            widths = SHIFT_WIDTHS
        else:
            w_comb = jnp.concatenate([w_in[l], w_in_vres[l - 1]], axis=1)
            mu = jnp.concatenate([mu_shift[l], mu_vres[l - 1]])
            widths = SHIFT_WIDTHS + (MV_LORA,)
        proj = hn @ w_comb

        pool_out = multiscale_pool(proj[..., :POOL_WIDTH], pool_w[l], pool_scale[l])

        s = token_shift(proj[..., POOL_WIDTH:].astype(f32), mu.astype(f32))
        parts = split_cols(s, widths)
        r, k, v, w_lo, a_lo, g_lo = parts[:6]
        w_log = -jax.nn.softplus(-(w0[l].astype(f32) + jnp.tanh(w_lo) @ w_lora_up[l].astype(f32))) - 0.5
        decay = jnp.exp(-jnp.exp(w_log))
        a = jax.nn.sigmoid(a0[l].astype(f32) + a_lo @ a_lora_up[l].astype(f32))
        g = jax.nn.sigmoid(g_lo) @ g_lora_up[l].astype(f32)
        if l == 0:
            v_first = v
        else:
            v_mix = jax.nn.sigmoid(v0[l - 1].astype(f32) + parts[6] @ v_lora_up[l - 1].astype(f32))
            v = v + (v_first - v) * v_mix
        kk = to_heads(k * k_k[l].astype(f32))
        kk = kk / jnp.maximum(jnp.linalg.norm(kk, axis=-1, keepdims=True), L2_EPS)
        k = k * (1.0 + (a - 1.0) * k_a[l].astype(f32))
        rh, kh, vh, ah = to_heads(r), to_heads(k), to_heads(v), to_heads(a)
        y = rwkv7_scan(rh, to_heads(decay), kh, vh, -kk, kk * ah)
        mean = jnp.mean(y, axis=-1, keepdims=True)
        var = jnp.mean(jnp.square(y - mean), axis=-1, keepdims=True)
        y = ((y - mean) * lax.rsqrt(var + GN_EPS)).reshape(bsz, T, RWKV_WIDTH)
        y = y * gn_w[l].astype(f32) + gn_b[l].astype(f32)
        bonus = jnp.sum(rh * kh * to_heads(r_k[l].astype(f32)), axis=-1, keepdims=True) * vh
        rwkv_out = ((y + bonus.reshape(bsz, T, RWKV_WIDTH)) * g).astype(dt)

        mixed = jnp.concatenate([pool_out, rwkv_out], axis=-1) @ w_out[l]
        h = h + mixed.astype(dt)

        hn2 = rms_norm(h, ln2_g[l])
        gate, up = jnp.split(hn2 @ w_gate_up[l], 2, axis=-1)
        h = h + ((jax.nn.silu(gate) * up) @ w_down[l]).astype(dt)

    out = rms_norm(h, final_g)
    return out[:, N_META:]
```

```python
import functools

import jax
import jax.numpy as jnp
from jax import lax
from jax.experimental import pallas as pl
from jax.experimental.pallas import tpu as pltpu

f32 = jnp.float32
bf16 = jnp.bfloat16

D_MODEL = 1024
N_META = 16
POOL_WIDTH = 512
POOL_WINDOWS = (2, 4, 8, 16)
POOL_GROUP = 128
RWKV_WIDTH = 512
HEAD_SIZE = 64
N_HEADS = 8
DECAY_LORA = 64
AAA_LORA = 64
MV_LORA = 32
GATE_LORA = 160
D_FF = 2816
RMS_EPS = 1e-6
GN_EPS = 64e-5
L2_EPS = 1e-12

C_R = POOL_WIDTH
C_K = C_R + RWKV_WIDTH
C_V = C_K + RWKV_WIDTH
C_LO1 = C_V + RWKV_WIDTH
C_LO2 = C_LO1 + DECAY_LORA + AAA_LORA
LO1_W = 128
LO2_W = 256
N_COMB = C_LO2 + LO2_W

CHUNK = 64
SEQ_PAD_TO = CHUNK * 33
HALO = 16
TT_MIX = 352
TT_SCAN = 704
TM_FFN = 512
FF_BLK = 704
VMEM_LIMIT = 56 * 1024 * 1024

_HI = lax.Precision.HIGHEST


def _dot(a, b):
    return jnp.dot(a, b, preferred_element_type=f32)


def _dot_nt(a, b):
    return lax.dot_general(a, b, (((1,), (1,)), ((), ())), preferred_element_type=f32)


def _dot_tn(a, b, precision=None):
    return lax.dot_general(a, b, (((0,), (0,)), ((), ())), precision=precision,
                           preferred_element_type=f32)


def _sigmoid(x):
    return 1.0 / (1.0 + jnp.exp(-x))


def _mix_in_kernel(*refs, has_vres, tt):
    if has_vres:
        (h_ref, g1_ref, w_ref, mu_ref, pw_ref, ps_ref, w0_ref, wup_ref, a0_ref, aup_ref,
         gup_ref, v0_ref, vup_ref, vf_ref,
         pool_ref, r_ref, k_ref, v_ref, lw_ref, a_ref, g_ref, e_ref) = refs
    else:
        (h_ref, g1_ref, w_ref, mu_ref, pw_ref, ps_ref, w0_ref, wup_ref, a0_ref, aup_ref,
         gup_ref,
         pool_ref, r_ref, k_ref, v_ref, lw_ref, a_ref, g_ref, e_ref) = refs
    j = pl.program_id(1)

    @pl.when(j == 0)
    def _():
        e_ref[0:HALO, :] = jnp.zeros((HALO, N_COMB), f32)

    x = h_ref[0]
    hn = x * lax.rsqrt(jnp.mean(x * x, axis=-1, keepdims=True) + RMS_EPS) * g1_ref[...]
    e_ref[HALO:HALO + tt, :] = _dot(hn.astype(bf16), w_ref[...])

    pos = j * tt + lax.broadcasted_iota(jnp.int32, (tt, 1), 0)
    for gi, win in enumerate(POOL_WINDOWS):
        c0 = gi * POOL_GROUP
        u_ext = e_ref[:, c0:c0 + POOL_GROUP]
        s = u_ext
        d = 1
        while d < win:
            s = s + pltpu.roll(s, d, axis=0)
            d *= 2
        cnt = jnp.minimum(pos + 1, win).astype(f32)
        diff = s[HALO:, :] / cnt - u_ext[HALO:, :]
        y = _dot(diff.astype(bf16), pw_ref[gi])
        pool_ref[0, :, c0:c0 + POOL_GROUP] = y * ps_ref[:, c0:c0 + POOL_GROUP]

    def shifted(c0, c1):
        cur = e_ref[HALO:HALO + tt, c0:c1]
        prev = e_ref[HALO - 1:HALO - 1 + tt, c0:c1]
        return cur + (prev - cur) * mu_ref[:, c0:c1]

    r_ref[0] = shifted(C_R, C_K)
    k_ref[0] = shifted(C_K, C_V)
    v = shifted(C_V, C_LO1)
    lo1 = shifted(C_LO1, C_LO2)
    lo2 = shifted(C_LO2, N_COMB)

    zw = w0_ref[...] + _dot(jnp.tanh(lo1).astype(bf16), wup_ref[...])
    lw_ref[0] = (-jnp.exp(f32(-0.5))) * _sigmoid(zw)
    a_ref[0] = _sigmoid(a0_ref[...] + _dot(lo1.astype(bf16), aup_ref[...]))
    g_ref[0] = _dot(_sigmoid(lo2).astype(bf16), gup_ref[...])
    if has_vres:
        v_mix = _sigmoid(v0_ref[...] + _dot(lo2.astype(bf16), vup_ref[...]))
        v = v + (vf_ref[0] - v) * v_mix
    v_ref[0] = v

    e_ref[0:HALO, :] = e_ref[tt:tt + HALO, :]


def _mix_in(h, g1, w_comb, mu, pool_w, pool_scale, w0, wup, a0, aup, gup, vres):
    bsz, tp, _ = h.shape
    tt = TT_MIX
    has_vres = vres is not None
    const2 = lambda b, j: (0, 0)
    tile = lambda width: pl.BlockSpec((1, tt, width), lambda b, j: (b, j, 0))
    cspec = lambda shape: pl.BlockSpec(shape, const2, pipeline_mode=pl.Buffered(1))
    in_specs = [
        tile(D_MODEL), cspec((1, D_MODEL)), cspec((D_MODEL, N_COMB)), cspec((1, N_COMB)),
        pl.BlockSpec((len(POOL_WINDOWS), POOL_GROUP, POOL_GROUP), lambda b, j: (0, 0, 0),
                     pipeline_mode=pl.Buffered(1)),
        cspec((1, POOL_WIDTH)), cspec((1, RWKV_WIDTH)), cspec((LO1_W, RWKV_WIDTH)),
        cspec((1, RWKV_WIDTH)), cspec((LO1_W, RWKV_WIDTH)), cspec((LO2_W, RWKV_WIDTH)),
    ]
    args = [h, g1, w_comb, mu, pool_w, pool_scale, w0, wup, a0, aup, gup]
    if has_vres:
        v0, vup, v_first = vres
        in_specs += [cspec((1, RWKV_WIDTH)), cspec((LO2_W, RWKV_WIDTH)), tile(RWKV_WIDTH)]
        args += [v0, vup, v_first]
    out_shape = tuple(jax.ShapeDtypeStruct((bsz, tp, RWKV_WIDTH), f32) for _ in range(7))
    return pl.pallas_call(
        functools.partial(_mix_in_kernel, has_vres=has_vres, tt=tt),
        out_shape=out_shape,
        grid=(bsz, tp // tt),
        in_specs=in_specs,
        out_specs=tuple(tile(RWKV_WIDTH) for _ in range(7)),
        scratch_shapes=[pltpu.VMEM((HALO + tt, N_COMB), f32)],
        compiler_params=pltpu.CompilerParams(
            dimension_semantics=("arbitrary", "arbitrary"), vmem_limit_bytes=VMEM_LIMIT),
        name="mix_in_vres" if has_vres else "mix_in",
    )(*args)


def _scan_kernel(r_ref, k_ref, v_ref, lw_ref, a_ref, g_ref, kk_ref, ka_ref, rk_ref,
                 gnw_ref, gnb_ref, o_ref, z_ref, *, ts):
    j = pl.program_id(1)

    @pl.when(j == 0)
    def _():
        z_ref[...] = jnp.zeros(z_ref.shape, f32)

    c = CHUNK
    row = lax.broadcasted_iota(jnp.int32, (c, c), 0)
    col = lax.broadcasted_iota(jnp.int32, (c, c), 1)
    tri_incl = (row >= col).astype(f32)
    ones_cc = jnp.ones((c, c), f32)
    row2 = lax.broadcasted_iota(jnp.int32, (2 * c, 2 * c), 0)
    col2 = lax.broadcasted_iota(jnp.int32, (2 * c, 2 * c), 1)
    rm, cm = row2 & (c - 1), col2 & (c - 1)
    a_mask = (rm > cm) | ((row2 >= c) & (rm == cm))
    row3 = lax.broadcasted_iota(jnp.int32, (c, 3 * c), 0)
    col3 = lax.broadcasted_iota(jnp.int32, (c, 3 * c), 1)
    zeros_cc = jnp.zeros((c, c), f32)

    def chunk_body(ci, carry):
        r0 = pl.multiple_of(ci * c, c)
        rows = pl.ds(r0, c)
        lw = lw_ref[0, rows, :]
        g_all = jnp.dot(tri_incl, lw, precision=_HI, preferred_element_type=f32)
        gcol_all = _dot_tn(lw, ones_cc, precision=_HI)
        for h in range(N_HEADS):
            hs = slice(h * HEAD_SIZE, (h + 1) * HEAD_SIZE)
            g = g_all[:, hs]
            lwh = lw[:, hs]
            r = r_ref[0, rows, hs]
            kraw = k_ref[0, rows, hs]
            v = v_ref[0, rows, hs]
            a_sig = a_ref[0, rows, hs]
            kkx = kraw * kk_ref[:, hs]
            ss = jnp.sum(kkx * kkx, axis=-1, keepdims=True)
            kk = kkx * lax.rsqrt(jnp.maximum(ss, L2_EPS * L2_EPS))
            kf = kraw * (1.0 + (a_sig - 1.0) * ka_ref[:, hs])
            eg = jnp.exp(g)
            eng = jnp.exp(-g)
            egc = jnp.exp(g[c - 1:c, :])
            rt = r * eg
            at = -kk * jnp.exp(g - lwh)
            kt = kf * eng
            bt = kk * a_sig * eng
            z = z_ref[h]
            zb = z.astype(bf16)
            vb = v.astype(bf16)

            amat = _dot_nt(jnp.concatenate([at, rt], axis=0).astype(bf16),
                           jnp.concatenate([bt, kt], axis=0).astype(bf16))
            amat = jnp.where(a_mask, amat, 0.0)
            w = jnp.concatenate([amat[0:c, :], at], axis=1)
            for lvl in range(6):
                same = (row3 >> (lvl + 1)) == (col3 >> (lvl + 1))
                dmat = jnp.where(same[:, 0:c], w[:, 0:c], 0.0)
                wm = jnp.where((col3 >= c) | jnp.logical_not(same), w, 0.0)
                w = wm + _dot(dmat.astype(bf16), wm.astype(bf16))
            u = _dot(w.astype(bf16),
                     jnp.concatenate([jnp.zeros((c, HEAD_SIZE), bf16), vb, zb], axis=0))
            ub = u.astype(bf16)
            y = _dot(jnp.concatenate([amat[c:2 * c, :], rt], axis=1).astype(bf16),
                     jnp.concatenate([ub, vb, zb], axis=0))
            z_new = jnp.exp(gcol_all[h * HEAD_SIZE:(h + 1) * HEAD_SIZE, :]) * z + _dot_tn(
                jnp.concatenate([bt * egc, kt * egc], axis=0).astype(bf16),
                jnp.concatenate([ub, vb], axis=0))
            z_ref[h] = z_new

            mean = jnp.mean(y, axis=-1, keepdims=True)
            yc = y - mean
            var = jnp.mean(yc * yc, axis=-1, keepdims=True)
            yn = yc * lax.rsqrt(var + GN_EPS) * gnw_ref[:, hs] + gnb_ref[:, hs]
            bonus = jnp.sum(r * kf * rk_ref[:, hs], axis=-1, keepdims=True) * v
            o_ref[0, rows, hs] = (yn + bonus) * g_ref[0, rows, hs]
        return carry

    lax.fori_loop(0, ts // c, chunk_body, 0)


def _scan(r, k, v, lw, a, g, k_k, k_a, r_k, gn_w, gn_b):
    bsz, tp, _ = r.shape
    ts = TT_SCAN
    tile = pl.BlockSpec((1, ts, RWKV_WIDTH), lambda b, j: (b, j, 0))
    cspec = pl.BlockSpec((1, RWKV_WIDTH), lambda b, j: (0, 0))
    return pl.pallas_call(
        functools.partial(_scan_kernel, ts=ts),
        out_shape=jax.ShapeDtypeStruct((bsz, tp, RWKV_WIDTH), f32),
        grid=(bsz, tp // ts),
        in_specs=[tile] * 6 + [cspec] * 5,
        out_specs=tile,
        scratch_shapes=[pltpu.VMEM((N_HEADS, HEAD_SIZE, HEAD_SIZE), f32)],
        compiler_params=pltpu.CompilerParams(
            dimension_semantics=("arbitrary", "arbitrary"), vmem_limit_bytes=VMEM_LIMIT),
        name="rwkv_scan",
    )(r, k, v, lw, a, g, k_k, k_a, r_k, gn_w, gn_b)


def _out_ffn_kernel(h_ref, pool_ref, rw_ref, wo_ref, g2_ref, wg_ref, wu_ref, wd_ref, gf_ref,
                    o_ref, *, final):
    mixed = (_dot(pool_ref[...].astype(bf16), wo_ref[0:POOL_WIDTH, :])
             + _dot(rw_ref[...].astype(bf16), wo_ref[POOL_WIDTH:, :]))
    h1 = h_ref[...] + mixed
    hn = h1 * lax.rsqrt(jnp.mean(h1 * h1, axis=-1, keepdims=True) + RMS_EPS) * g2_ref[...]
    hb = hn.astype(bf16)
    acc = h1
    for blk in range(D_FF // FF_BLK):
        cs = slice(blk * FF_BLK, (blk + 1) * FF_BLK)
        gate = _dot(hb, wg_ref[:, cs])
        up = _dot(hb, wu_ref[:, cs])
        act = gate * _sigmoid(gate) * up
        acc = acc + _dot(act.astype(bf16), wd_ref[cs, :])
    if final:
        acc = acc * lax.rsqrt(jnp.mean(acc * acc, axis=-1, keepdims=True) + RMS_EPS) * gf_ref[...]
    o_ref[...] = acc


def _out_ffn(h, pool, rw, w_out, g2, w_gate, w_up, w_down, g_final, final):
    rows = h.shape[0]
    tm = TM_FFN
    tile = lambda width: pl.BlockSpec((tm, width), lambda i: (i, 0))
    cspec = lambda shape: pl.BlockSpec(shape, lambda i: (0, 0), pipeline_mode=pl.Buffered(1))
    return pl.pallas_call(
        functools.partial(_out_ffn_kernel, final=final),
        out_shape=jax.ShapeDtypeStruct((rows, D_MODEL), f32),
        grid=(rows // tm,),
        in_specs=[tile(D_MODEL), tile(POOL_WIDTH), tile(RWKV_WIDTH),
                  cspec((D_MODEL, D_MODEL)), cspec((1, D_MODEL)),
                  cspec((D_MODEL, D_FF)), cspec((D_MODEL, D_FF)), cspec((D_FF, D_MODEL)),
                  cspec((1, D_MODEL))],
        out_specs=tile(D_MODEL),
        compiler_params=pltpu.CompilerParams(
            dimension_semantics=("arbitrary",), vmem_limit_bytes=VMEM_LIMIT),
        name="out_ffn_final" if final else "out_ffn",
    )(h, pool, rw, w_out, g2, w_gate, w_up, w_down, g_final)


def _row(p):
    return p.reshape(1, -1).astype(f32)


def _pad_rows(w, before, total):
    return jnp.pad(w, ((before, total - before - w.shape[0]), (0, 0)))


def kernel(x, meta_tokens, ln1_g, w_in, mu_shift, pool_w, pool_scale, w0, w_lora_up, a0, a_lora_up, g_lora_up, k_k, k_a, r_k, gn_w, gn_b, w_in_vres, mu_vres, v0, v_lora_up, w_out, ln2_g, w_gate_up, w_down, final_g):
    bsz, seq, _ = x.shape
    depth = w_in.shape[0]
    t_real = N_META + seq
    tp = SEQ_PAD_TO
    assert t_real <= tp and tp % TT_MIX == 0 and tp % TT_SCAN == 0 and (bsz * tp) % TM_FFN == 0
    meta = jnp.broadcast_to(meta_tokens.astype(x.dtype)[None], (bsz, N_META, D_MODEL))
    h = jnp.concatenate([meta, x, jnp.zeros((bsz, tp - t_real, D_MODEL), x.dtype)], axis=1)

    v_first = None
    for l in range(depth):
        n_in = w_in.shape[2]
        w_cols = [w_in[l]]
        mu_cols = [jnp.zeros((POOL_WIDTH,), f32), mu_shift[l].astype(f32)]
        used = n_in
        if l > 0:
            w_cols.append(w_in_vres[l - 1])
            mu_cols.append(mu_vres[l - 1].astype(f32))
            used += MV_LORA
        w_comb = jnp.pad(jnp.concatenate(w_cols, axis=1), ((0, 0), (0, N_COMB - used))).astype(bf16)
        mu = jnp.pad(jnp.concatenate(mu_cols), (0, N_COMB - used)).reshape(1, N_COMB)
        wup = _pad_rows(w_lora_up[l], 0, LO1_W).astype(bf16)
        aup = _pad_rows(a_lora_up[l], DECAY_LORA, LO1_W).astype(bf16)
        gup = _pad_rows(g_lora_up[l], 0, LO2_W).astype(bf16)
        vres = None
        if l > 0:
            vup = _pad_rows(v_lora_up[l - 1], GATE_LORA, LO2_W).astype(bf16)
            vres = (_row(v0[l - 1]), vup, v_first)
        pool_o, r, k, v, lw, a, g = _mix_in(
            h, _row(ln1_g[l]), w_comb, mu, pool_w[l].astype(bf16), _row(pool_scale[l]),
            _row(w0[l]), wup, _row(a0[l]), aup, gup, vres)
        if l == 0:
            v_first = v
        rw = _scan(r, k, v, lw, a, g, _row(k_k[l]), _row(k_a[l]), _row(r_k[l]),
                   _row(gn_w[l]), _row(gn_b[l]))
        wgu = w_gate_up[l].astype(bf16)
        h = _out_ffn(h.reshape(bsz * tp, D_MODEL), pool_o.reshape(bsz * tp, POOL_WIDTH),
                     rw.reshape(bsz * tp, RWKV_WIDTH), w_out[l].astype(bf16), _row(ln2_g[l]),
                     wgu[:, :D_FF], wgu[:, D_FF:], w_down[l].astype(bf16), _row(final_g),
                     final=(l == depth - 1)).reshape(bsz, tp, D_MODEL)
    return h[:, N_META:t_real]
```

```python
import functools

import jax
import jax.numpy as jnp
from jax import lax
from jax.experimental import pallas as pl
from jax.experimental.pallas import tpu as pltpu

f32 = jnp.float32
bf16 = jnp.bfloat16

D_MODEL = 1024
N_META = 16
POOL_WIDTH = 512
POOL_WINDOWS = (2, 4, 8, 16)
POOL_GROUP = 128
RWKV_WIDTH = 512
HEAD_SIZE = 64
N_HEADS = 8
DECAY_LORA = 64
AAA_LORA = 64
MV_LORA = 32
GATE_LORA = 160
D_FF = 2816
RMS_EPS = 1e-6
GN_EPS = 64e-5
L2_EPS = 1e-12

C_R = POOL_WIDTH
C_K = C_R + RWKV_WIDTH
C_V = C_K + RWKV_WIDTH
C_LO1 = C_V + RWKV_WIDTH
C_LO2 = C_LO1 + DECAY_LORA + AAA_LORA
LO1_W = 128
LO2_W = 256
N_COMB = C_LO2 + LO2_W

CHUNK = 64
SEQ_PAD_TO = CHUNK * 33
HALO = 16
TT_MIX = 352
TT_SCAN = 704
TM_FFN = 512
FF_BLK = 704
VMEM_LIMIT = 56 * 1024 * 1024

_HI = lax.Precision.HIGHEST


def _dot(a, b):
    return jnp.dot(a, b, preferred_element_type=f32)


def _dot_nt(a, b):
    return lax.dot_general(a, b, (((1,), (1,)), ((), ())), preferred_element_type=f32)


def _dot_tn(a, b, precision=None):
    return lax.dot_general(a, b, (((0,), (0,)), ((), ())), precision=precision,
                           preferred_element_type=f32)


def _sigmoid(x):
    return 1.0 / (1.0 + jnp.exp(-x))


def _mix_in_kernel(*refs, has_vres, tt):
    if has_vres:
        (h_ref, g1_ref, w_ref, mu_ref, pw_ref, ps_ref, w0_ref, wup_ref, a0_ref, aup_ref,
         gup_ref, v0_ref, vup_ref, vf_ref,
         pool_ref, r_ref, k_ref, v_ref, lw_ref, a_ref, g_ref, e_ref) = refs
    else:
        (h_ref, g1_ref, w_ref, mu_ref, pw_ref, ps_ref, w0_ref, wup_ref, a0_ref, aup_ref,
         gup_ref,
         pool_ref, r_ref, k_ref, v_ref, lw_ref, a_ref, g_ref, e_ref) = refs
    j = pl.program_id(1)

    @pl.when(j == 0)
    def _():
        e_ref[0:HALO, :] = jnp.zeros((HALO, N_COMB), f32)

    x = h_ref[0]
    hn = x * lax.rsqrt(jnp.mean(x * x, axis=-1, keepdims=True) + RMS_EPS) * g1_ref[...]
    e_ref[HALO:HALO + tt, :] = _dot(hn.astype(bf16), w_ref[...])

    pos = j * tt + lax.broadcasted_iota(jnp.int32, (tt, 1), 0)
    for gi, win in enumerate(POOL_WINDOWS):
        c0 = gi * POOL_GROUP
        u_ext = e_ref[:, c0:c0 + POOL_GROUP]
        s = u_ext
        d = 1
        while d < win:
            s = s + pltpu.roll(s, d, axis=0)
            d *= 2
        cnt = jnp.minimum(pos + 1, win).astype(f32)
        diff = s[HALO:, :] / cnt - u_ext[HALO:, :]
        y = _dot(diff.astype(bf16), pw_ref[gi])
        pool_ref[0, :, c0:c0 + POOL_GROUP] = y * ps_ref[:, c0:c0 + POOL_GROUP]

    def shifted(c0, c1):
        cur = e_ref[HALO:HALO + tt, c0:c1]
        prev = e_ref[HALO - 1:HALO - 1 + tt, c0:c1]
        return cur + (prev - cur) * mu_ref[:, c0:c1]

    r_ref[0] = shifted(C_R, C_K)
    k_ref[0] = shifted(C_K, C_V)
    v = shifted(C_V, C_LO1)
    lo1 = shifted(C_LO1, C_LO2)
    lo2 = shifted(C_LO2, N_COMB)

    zw = w0_ref[...] + _dot(jnp.tanh(lo1).astype(bf16), wup_ref[...])
    lw_ref[0] = (-jnp.exp(f32(-0.5))) * _sigmoid(zw)
    a_ref[0] = _sigmoid(a0_ref[...] + _dot(lo1.astype(bf16), aup_ref[...]))
    g_ref[0] = _dot(_sigmoid(lo2).astype(bf16), gup_ref[...])
    if has_vres:
        v_mix = _sigmoid(v0_ref[...] + _dot(lo2.astype(bf16), vup_ref[...]))
        v = v + (vf_ref[0] - v) * v_mix
    v_ref[0] = v

    e_ref[0:HALO, :] = e_ref[tt:tt + HALO, :]


def _mix_in(h, g1, w_comb, mu, pool_w, pool_scale, w0, wup, a0, aup, gup, vres):
    bsz, tp, _ = h.shape
    tt = TT_MIX
    has_vres = vres is not None
    const2 = lambda b, j: (0, 0)
    tile = lambda width: pl.BlockSpec((1, tt, width), lambda b, j: (b, j, 0))
    cspec = lambda shape: pl.BlockSpec(shape, const2, pipeline_mode=pl.Buffered(1))
    in_specs = [
        tile(D_MODEL), cspec((1, D_MODEL)), cspec((D_MODEL, N_COMB)), cspec((1, N_COMB)),
        pl.BlockSpec((len(POOL_WINDOWS), POOL_GROUP, POOL_GROUP), lambda b, j: (0, 0, 0),
                     pipeline_mode=pl.Buffered(1)),
        cspec((1, POOL_WIDTH)), cspec((1, RWKV_WIDTH)), cspec((LO1_W, RWKV_WIDTH)),
        cspec((1, RWKV_WIDTH)), cspec((LO1_W, RWKV_WIDTH)), cspec((LO2_W, RWKV_WIDTH)),
    ]
    args = [h, g1, w_comb, mu, pool_w, pool_scale, w0, wup, a0, aup, gup]
    if has_vres:
        v0, vup, v_first = vres
        in_specs += [cspec((1, RWKV_WIDTH)), cspec((LO2_W, RWKV_WIDTH)), tile(RWKV_WIDTH)]
        args += [v0, vup, v_first]
    out_shape = tuple(jax.ShapeDtypeStruct((bsz, tp, RWKV_WIDTH), f32) for _ in range(7))
    return pl.pallas_call(
        functools.partial(_mix_in_kernel, has_vres=has_vres, tt=tt),
        out_shape=out_shape,
        grid=(bsz, tp // tt),
        in_specs=in_specs,
        out_specs=tuple(tile(RWKV_WIDTH) for _ in range(7)),
        scratch_shapes=[pltpu.VMEM((HALO + tt, N_COMB), f32)],
        compiler_params=pltpu.CompilerParams(
            dimension_semantics=("arbitrary", "arbitrary"), vmem_limit_bytes=VMEM_LIMIT),
        name="mix_in_vres" if has_vres else "mix_in",
    )(*args)


def _scan_kernel(r_ref, k_ref, v_ref, lw_ref, a_ref, g_ref, kk_ref, ka_ref, rk_ref,
                 gnw_ref, gnb_ref, o_ref, z_ref, *, ts):
    j = pl.program_id(1)

    @pl.when(j == 0)
    def _():
        z_ref[...] = jnp.zeros(z_ref.shape, f32)

    c = CHUNK
    row = lax.broadcasted_iota(jnp.int32, (c, c), 0)
    col = lax.broadcasted_iota(jnp.int32, (c, c), 1)
    tri_incl = (row >= col).astype(f32)
    ones_cc = jnp.ones((c, c), f32)
    row2 = lax.broadcasted_iota(jnp.int32, (2 * c, 2 * c), 0)
    col2 = lax.broadcasted_iota(jnp.int32, (2 * c, 2 * c), 1)
    rm, cm = row2 & (c - 1), col2 & (c - 1)
    a_mask = (rm > cm) | ((row2 >= c) & (rm == cm))
    row3 = lax.broadcasted_iota(jnp.int32, (c, 3 * c), 0)
    col3 = lax.broadcasted_iota(jnp.int32, (c, 3 * c), 1)
    zeros_cc = jnp.zeros((c, c), f32)

    def chunk_body(ci, carry):
        r0 = pl.multiple_of(ci * c, c)
        rows = pl.ds(r0, c)
        lw = lw_ref[0, rows, :]
        g_all = jnp.dot(tri_incl, lw, precision=_HI, preferred_element_type=f32)
        gcol_all = _dot_tn(lw, ones_cc, precision=_HI)
        heads = range(N_HEADS)
        hsl = [slice(h * HEAD_SIZE, (h + 1) * HEAD_SIZE) for h in heads]
        rs, vs, kfs, ats, rts, bhs, khs, vbs, zs, zbs, amats, ws = ([] for _ in range(12))
        for h in heads:
            hs = hsl[h]
            g = g_all[:, hs]
            r = r_ref[0, rows, hs]
            kraw = k_ref[0, rows, hs]
            v = v_ref[0, rows, hs]
            a_sig = a_ref[0, rows, hs]
            kkx = kraw * kk_ref[:, hs]
            ss = jnp.sum(kkx * kkx, axis=-1, keepdims=True)
            kk = kkx * lax.rsqrt(jnp.maximum(ss, L2_EPS * L2_EPS))
            kf = kraw * (1.0 + (a_sig - 1.0) * ka_ref[:, hs])
            eng = jnp.exp(-g)
            egc = jnp.exp(g[c - 1:c, :])
            rt = r * jnp.exp(g)
            at = -kk * jnp.exp(g - lw[:, hs])
            kt = kf * eng
            bt = kk * a_sig * eng
            amat = _dot_nt(jnp.concatenate([at, rt], axis=0).astype(bf16),
                           jnp.concatenate([bt, kt], axis=0).astype(bf16))
            amat = jnp.where(a_mask, amat, 0.0)
            z = z_ref[h]
            rs.append(r); vs.append(v); kfs.append(kf); ats.append(at); rts.append(rt)
            bhs.append(bt * egc); khs.append(kt * egc)
            vbs.append(v.astype(bf16)); zs.append(z); zbs.append(z.astype(bf16))
            amats.append(amat)
            ws.append(jnp.concatenate([amat[0:c, :], at], axis=1))
        for lvl in range(6):
            same = (row3 >> (lvl + 1)) == (col3 >> (lvl + 1))
            keep = (col3 >= c) | jnp.logical_not(same)
            nxt = []
            for h in heads:
                dmat = jnp.where(same[:, 0:c], ws[h][:, 0:c], 0.0)
                wm = jnp.where(keep, ws[h], 0.0)
                nxt.append(wm + _dot(dmat.astype(bf16), wm.astype(bf16)))
            ws = nxt
        zero_rows = jnp.zeros((c, HEAD_SIZE), bf16)
        ubs = [_dot(ws[h].astype(bf16),
                    jnp.concatenate([zero_rows, vbs[h], zbs[h]], axis=0)).astype(bf16)
               for h in heads]
        ys = [_dot(jnp.concatenate([amats[h][c:2 * c, :], rts[h]], axis=1).astype(bf16),
                   jnp.concatenate([ubs[h], vbs[h], zbs[h]], axis=0)) for h in heads]
        for h in heads:
            z_ref[h] = jnp.exp(gcol_all[h * HEAD_SIZE:(h + 1) * HEAD_SIZE, :]) * zs[h] + _dot_tn(
                jnp.concatenate([bhs[h], khs[h]], axis=0).astype(bf16),
                jnp.concatenate([ubs[h], vbs[h]], axis=0))
        for h in heads:
            hs = hsl[h]
            y = ys[h]
            mean = jnp.mean(y, axis=-1, keepdims=True)
            yc = y - mean
            var = jnp.mean(yc * yc, axis=-1, keepdims=True)
            yn = yc * lax.rsqrt(var + GN_EPS) * gnw_ref[:, hs] + gnb_ref[:, hs]
            bonus = jnp.sum(rs[h] * kfs[h] * rk_ref[:, hs], axis=-1, keepdims=True) * vs[h]
            o_ref[0, rows, hs] = (yn + bonus) * g_ref[0, rows, hs]
        return carry

    lax.fori_loop(0, ts // c, chunk_body, 0)


def _scan(r, k, v, lw, a, g, k_k, k_a, r_k, gn_w, gn_b):
    bsz, tp, _ = r.shape
    ts = TT_SCAN
    tile = pl.BlockSpec((1, ts, RWKV_WIDTH), lambda b, j: (b, j, 0))
    cspec = pl.BlockSpec((1, RWKV_WIDTH), lambda b, j: (0, 0))
    return pl.pallas_call(
        functools.partial(_scan_kernel, ts=ts),
        out_shape=jax.ShapeDtypeStruct((bsz, tp, RWKV_WIDTH), f32),
        grid=(bsz, tp // ts),
        in_specs=[tile] * 6 + [cspec] * 5,
        out_specs=tile,
        scratch_shapes=[pltpu.VMEM((N_HEADS, HEAD_SIZE, HEAD_SIZE), f32)],
        compiler_params=pltpu.CompilerParams(
            dimension_semantics=("arbitrary", "arbitrary"), vmem_limit_bytes=VMEM_LIMIT),
        name="rwkv_scan",
    )(r, k, v, lw, a, g, k_k, k_a, r_k, gn_w, gn_b)


def _out_ffn_kernel(h_ref, pool_ref, rw_ref, wo_ref, g2_ref, wg_ref, wu_ref, wd_ref, gf_ref,
                    o_ref, *, final):
    mixed = (_dot(pool_ref[...].astype(bf16), wo_ref[0:POOL_WIDTH, :])
             + _dot(rw_ref[...].astype(bf16), wo_ref[POOL_WIDTH:, :]))
    h1 = h_ref[...] + mixed
    hn = h1 * lax.rsqrt(jnp.mean(h1 * h1, axis=-1, keepdims=True) + RMS_EPS) * g2_ref[...]
    hb = hn.astype(bf16)
    acc = h1
    for blk in range(D_FF // FF_BLK):
        cs = slice(blk * FF_BLK, (blk + 1) * FF_BLK)
        gate = _dot(hb, wg_ref[:, cs])
        up = _dot(hb, wu_ref[:, cs])
        act = gate * _sigmoid(gate) * up
        acc = acc + _dot(act.astype(bf16), wd_ref[cs, :])
    if final:
        acc = acc * lax.rsqrt(jnp.mean(acc * acc, axis=-1, keepdims=True) + RMS_EPS) * gf_ref[...]
    o_ref[...] = acc


def _out_ffn(h, pool, rw, w_out, g2, w_gate, w_up, w_down, g_final, final):
    rows = h.shape[0]
    tm = TM_FFN
    tile = lambda width: pl.BlockSpec((tm, width), lambda i: (i, 0))
    cspec = lambda shape: pl.BlockSpec(shape, lambda i: (0, 0), pipeline_mode=pl.Buffered(1))
    return pl.pallas_call(
        functools.partial(_out_ffn_kernel, final=final),
        out_shape=jax.ShapeDtypeStruct((rows, D_MODEL), f32),
        grid=(rows // tm,),
        in_specs=[tile(D_MODEL), tile(POOL_WIDTH), tile(RWKV_WIDTH),
                  cspec((D_MODEL, D_MODEL)), cspec((1, D_MODEL)),
                  cspec((D_MODEL, D_FF)), cspec((D_MODEL, D_FF)), cspec((D_FF, D_MODEL)),
                  cspec((1, D_MODEL))],
        out_specs=tile(D_MODEL),
        compiler_params=pltpu.CompilerParams(
            dimension_semantics=("arbitrary",), vmem_limit_bytes=VMEM_LIMIT),
        name="out_ffn_final" if final else "out_ffn",
    )(h, pool, rw, w_out, g2, w_gate, w_up, w_down, g_final)


def _row(p):
    return p.reshape(1, -1).astype(f32)


def _pad_rows(w, before, total):
    return jnp.pad(w, ((before, total - before - w.shape[0]), (0, 0)))


def kernel(x, meta_tokens, ln1_g, w_in, mu_shift, pool_w, pool_scale, w0, w_lora_up, a0, a_lora_up, g_lora_up, k_k, k_a, r_k, gn_w, gn_b, w_in_vres, mu_vres, v0, v_lora_up, w_out, ln2_g, w_gate_up, w_down, final_g):
    bsz, seq, _ = x.shape
    depth = w_in.shape[0]
    t_real = N_META + seq
    tp = SEQ_PAD_TO
    assert t_real <= tp and tp % TT_MIX == 0 and tp % TT_SCAN == 0 and (bsz * tp) % TM_FFN == 0
    meta = jnp.broadcast_to(meta_tokens.astype(x.dtype)[None], (bsz, N_META, D_MODEL))
    h = jnp.concatenate([meta, x, jnp.zeros((bsz, tp - t_real, D_MODEL), x.dtype)], axis=1)

    v_first = None
    for l in range(depth):
        n_in = w_in.shape[2]
        w_cols = [w_in[l]]
        mu_cols = [jnp.zeros((POOL_WIDTH,), f32), mu_shift[l].astype(f32)]
        used = n_in
        if l > 0:
            w_cols.append(w_in_vres[l - 1])
            mu_cols.append(mu_vres[l - 1].astype(f32))
            used += MV_LORA
        w_comb = jnp.pad(jnp.concatenate(w_cols, axis=1), ((0, 0), (0, N_COMB - used))).astype(bf16)
        mu = jnp.pad(jnp.concatenate(mu_cols), (0, N_COMB - used)).reshape(1, N_COMB)
        wup = _pad_rows(w_lora_up[l], 0, LO1_W).astype(bf16)
        aup = _pad_rows(a_lora_up[l], DECAY_LORA, LO1_W).astype(bf16)
        gup = _pad_rows(g_lora_up[l], 0, LO2_W).astype(bf16)
        vres = None
        if l > 0:
            vup = _pad_rows(v_lora_up[l - 1], GATE_LORA, LO2_W).astype(bf16)
            vres = (_row(v0[l - 1]), vup, v_first)
        pool_o, r, k, v, lw, a, g = _mix_in(
            h, _row(ln1_g[l]), w_comb, mu, pool_w[l].astype(bf16), _row(pool_scale[l]),
            _row(w0[l]), wup, _row(a0[l]), aup, gup, vres)
        if l == 0:
            v_first = v
        rw = _scan(r, k, v, lw, a, g, _row(k_k[l]), _row(k_a[l]), _row(r_k[l]),
                   _row(gn_w[l]), _row(gn_b[l]))
        wgu = w_gate_up[l].astype(bf16)
        h = _out_ffn(h.reshape(bsz * tp, D_MODEL), pool_o.reshape(bsz * tp, POOL_WIDTH),
                     rw.reshape(bsz * tp, RWKV_WIDTH), w_out[l].astype(bf16), _row(ln2_g[l]),
                     wgu[:, :D_FF], wgu[:, D_FF:], w_down[l].astype(bf16), _row(final_g),
                     final=(l == depth - 1)).reshape(bsz, tp, D_MODEL)
    return h[:, N_META:t_real]
```

```python
import functools

import jax
import jax.numpy as jnp
from jax import lax
from jax.experimental import pallas as pl
from jax.experimental.pallas import tpu as pltpu

f32 = jnp.float32
bf16 = jnp.bfloat16

D_MODEL = 1024
N_META = 16
POOL_WIDTH = 512
POOL_WINDOWS = (2, 4, 8, 16)
POOL_GROUP = 128
RWKV_WIDTH = 512
HEAD_SIZE = 64
N_HEADS = 8
DECAY_LORA = 64
AAA_LORA = 64
MV_LORA = 32
GATE_LORA = 160
D_FF = 2816
RMS_EPS = 1e-6
GN_EPS = 64e-5
L2_EPS = 1e-12

C_R = POOL_WIDTH
C_K = C_R + RWKV_WIDTH
C_V = C_K + RWKV_WIDTH
C_LO1 = C_V + RWKV_WIDTH
C_LO2 = C_LO1 + DECAY_LORA + AAA_LORA
LO1_W = 128
LO2_W = 256
N_COMB = C_LO2 + LO2_W

CHUNK = 64
SEQ_PAD_TO = CHUNK * 33
HALO = 16
SUBLANES = 8
TT_MIX = 3 * CHUNK
TT_SCAN = 11 * CHUNK
TM_FFN = 512
FF_BLK = 704
VMEM_LIMIT = 56 * 1024 * 1024

_HEAD_SLICES = [slice(h * HEAD_SIZE, (h + 1) * HEAD_SIZE) for h in range(N_HEADS)]


def _dot(a, b):
    return jnp.dot(a, b, preferred_element_type=f32)


def _dot_nt(a, b):
    return lax.dot_general(a, b, (((1,), (1,)), ((), ())), preferred_element_type=f32)


def _sigmoid(x):
    return 1.0 / (1.0 + jnp.exp(-x))


def _head_sums(x):
    return [jnp.sum(x[:, hs], axis=-1, keepdims=True) for hs in _HEAD_SLICES]


def _head_bcast(cols, low_half):
    return jnp.concatenate(
        [jnp.where(low_half, cols[2 * p], cols[2 * p + 1]) for p in range(N_HEADS // 2)], axis=1)


def _mix_in_kernel(*refs, has_vres, tt, n_tiles, tiles_per_row):
    if has_vres:
        (h_ref, g1_ref, w_ref, mu_ref, pw_ref, ps_ref, w0_ref, wup_ref, a0_ref, aup_ref,
         gup_ref, kk_ref, ka_ref, rk_ref, v0_ref, vup_ref, vf_ref,
         pool_ref, at_ref, rt_ref, bt_ref, kt_ref, bh_ref, kh_ref, vb_ref, bonus_ref, g_ref,
         egc_ref, e_ref, stash_ref) = refs
        vout_ref = None
    else:
        (h_ref, g1_ref, w_ref, mu_ref, pw_ref, ps_ref, w0_ref, wup_ref, a0_ref, aup_ref,
         gup_ref, kk_ref, ka_ref, rk_ref,
         pool_ref, at_ref, rt_ref, bt_ref, kt_ref, bh_ref, kh_ref, vb_ref, bonus_ref, g_ref,
         egc_ref, vout_ref, e_ref, stash_ref) = refs
    s = pl.program_id(0)
    j = jnp.minimum(s, n_tiles - 1) % tiles_per_row
    c = CHUNK

    @pl.when(s == 0)
    def _():
        stash_ref[...] = jnp.zeros(stash_ref.shape, f32)

    @pl.when(j == 0)
    def _():
        e_ref[0:HALO, :] = jnp.zeros((HALO, N_COMB), f32)

    prev = stash_ref.at[(s + 1) % 2]
    r, kraw, v, lw, a_sig = prev[0], prev[1], prev[2], prev[3], prev[4]
    row = lax.broadcasted_iota(jnp.int32, (c, c), 0)
    col = lax.broadcasted_iota(jnp.int32, (c, c), 1)
    tri_incl = (row >= col).astype(bf16)
    lw_hi = lw.astype(bf16)
    lw_r1 = lw - lw_hi.astype(f32)
    lw_mid = lw_r1.astype(bf16)
    lw_lo = (lw_r1 - lw_mid.astype(f32)).astype(bf16)
    g_parts, egc_parts = [], []
    for ci in range(tt // c):
        rows = slice(ci * c, (ci + 1) * c)
        g_ci = _dot(tri_incl, lw_hi[rows]) + _dot(tri_incl, lw_mid[rows]) + _dot(tri_incl, lw_lo[rows])
        egc = jnp.exp(g_ci[c - 1:c, :])
        egc_ref[0, ci * SUBLANES:(ci + 1) * SUBLANES, :] = jnp.broadcast_to(egc, (SUBLANES, RWKV_WIDTH))
        g_parts.append(g_ci)
        egc_parts.append(jnp.broadcast_to(egc, (c, RWKV_WIDTH)))
    g = jnp.concatenate(g_parts, axis=0)
    egc_rows = jnp.concatenate(egc_parts, axis=0)
    low_half = lax.broadcasted_iota(jnp.int32, (tt, 2 * HEAD_SIZE), 1) < HEAD_SIZE
    kkx = kraw * kk_ref[...]
    inv_norm = [lax.rsqrt(jnp.maximum(ss, L2_EPS * L2_EPS)) for ss in _head_sums(kkx * kkx)]
    kk = kkx * _head_bcast(inv_norm, low_half)
    kf = kraw * (1.0 + (a_sig - 1.0) * ka_ref[...])
    eng = jnp.exp(-g)
    kt = kf * eng
    bt = kk * a_sig * eng
    rt_ref[0] = (r * jnp.exp(g)).astype(bf16)
    at_ref[0] = (-(kk * jnp.exp(g - lw))).astype(bf16)
    kt_ref[0] = kt.astype(bf16)
    bt_ref[0] = bt.astype(bf16)
    kh_ref[0] = (kt * egc_rows).astype(bf16)
    bh_ref[0] = (bt * egc_rows).astype(bf16)
    vb_ref[0] = v.astype(bf16)
    bonus_ref[0] = _head_bcast(_head_sums(r * kf * rk_ref[...]), low_half) * v

    x = h_ref[0]
    hn = x * lax.rsqrt(jnp.mean(x * x, axis=-1, keepdims=True) + RMS_EPS) * g1_ref[...]
    e_ref[HALO:HALO + tt, :] = _dot(hn.astype(bf16), w_ref[...])

    pos = j * tt + lax.broadcasted_iota(jnp.int32, (tt, 1), 0)
    for gi, win in enumerate(POOL_WINDOWS):
        c0 = gi * POOL_GROUP
        u_ext = e_ref[:, c0:c0 + POOL_GROUP]
        acc = u_ext
        d = 1
        while d < win:
            acc = acc + pltpu.roll(acc, d, axis=0)
            d *= 2
        cnt = jnp.minimum(pos + 1, win).astype(f32)
        diff = acc[HALO:, :] / cnt - u_ext[HALO:, :]
        y = _dot(diff.astype(bf16), pw_ref[gi])
        pool_ref[0, :, c0:c0 + POOL_GROUP] = y * ps_ref[:, c0:c0 + POOL_GROUP]

    def shifted(c0, c1):
        cur = e_ref[HALO:HALO + tt, c0:c1]
        before = e_ref[HALO - 1:HALO - 1 + tt, c0:c1]
        return cur + (before - cur) * mu_ref[:, c0:c1]

    nxt = stash_ref.at[s % 2]
    nxt[0] = shifted(C_R, C_K)
    nxt[1] = shifted(C_K, C_V)
    v_new = shifted(C_V, C_LO1)
    lo1 = shifted(C_LO1, C_LO2)
    lo2 = shifted(C_LO2, N_COMB)
    zw = w0_ref[...] + _dot(jnp.tanh(lo1).astype(bf16), wup_ref[...])
    nxt[3] = (-jnp.exp(f32(-0.5))) * _sigmoid(zw)
    nxt[4] = _sigmoid(a0_ref[...] + _dot(lo1.astype(bf16), aup_ref[...]))
    g_ref[0] = _dot(_sigmoid(lo2).astype(bf16), gup_ref[...])
    if has_vres:
        v_mix = _sigmoid(v0_ref[...] + _dot(lo2.astype(bf16), vup_ref[...]))
        v_new = v_new + (vf_ref[0] - v_new) * v_mix
    else:
        vout_ref[0] = v_new
    nxt[2] = v_new

    e_ref[0:HALO, :] = jnp.where(s < n_tiles - 1, e_ref[tt:tt + HALO, :], e_ref[0:HALO, :])


def _mix_in(h, g1, w_comb, mu, pool_w, pool_scale, w0, wup, a0, aup, gup, k_k, k_a, r_k, vres):
    bsz, tp, _ = h.shape
    tt = TT_MIX
    tiles_per_row = tp // tt
    n_tiles = bsz * tiles_per_row
    egc_rows = tt // CHUNK * SUBLANES
    has_vres = vres is not None

    def cur_map(s):
        t = jnp.minimum(s, n_tiles - 1)
        return (t // tiles_per_row, t % tiles_per_row, 0)

    def lag_map(s):
        t = jnp.maximum(s - 1, 0)
        return (t // tiles_per_row, t % tiles_per_row, 0)

    cur = lambda width: pl.BlockSpec((1, tt, width), cur_map)
    lag = lambda width: pl.BlockSpec((1, tt, width), lag_map)
    cspec = lambda shape: pl.BlockSpec(shape, lambda s: (0,) * len(shape), pipeline_mode=pl.Buffered(1))
    in_specs = [
        cur(D_MODEL), cspec((1, D_MODEL)), cspec((D_MODEL, N_COMB)), cspec((1, N_COMB)),
        cspec((len(POOL_WINDOWS), POOL_GROUP, POOL_GROUP)),
        cspec((1, POOL_WIDTH)), cspec((1, RWKV_WIDTH)), cspec((LO1_W, RWKV_WIDTH)),
        cspec((1, RWKV_WIDTH)), cspec((LO1_W, RWKV_WIDTH)), cspec((LO2_W, RWKV_WIDTH)),
        cspec((1, RWKV_WIDTH)), cspec((1, RWKV_WIDTH)), cspec((1, RWKV_WIDTH)),
    ]
    args = [h, g1, w_comb, mu, pool_w, pool_scale, w0, wup, a0, aup, gup, k_k, k_a, r_k]
    wide = lambda dtype: jax.ShapeDtypeStruct((bsz, tp, RWKV_WIDTH), dtype)
    out_shape = [wide(f32)] + [wide(bf16)] * 7 + [wide(f32), wide(f32),
                 jax.ShapeDtypeStruct((bsz, tp // CHUNK * SUBLANES, RWKV_WIDTH), f32)]
    out_specs = [cur(RWKV_WIDTH)] + [lag(RWKV_WIDTH)] * 8 + [
        cur(RWKV_WIDTH), pl.BlockSpec((1, egc_rows, RWKV_WIDTH), lag_map)]
    if has_vres:
        v0, vup, v_first = vres
        in_specs += [cspec((1, RWKV_WIDTH)), cspec((LO2_W, RWKV_WIDTH)), cur(RWKV_WIDTH)]
        args += [v0, vup, v_first]
    else:
        out_shape.append(wide(f32))
        out_specs.append(cur(RWKV_WIDTH))
    return pl.pallas_call(
        functools.partial(_mix_in_kernel, has_vres=has_vres, tt=tt, n_tiles=n_tiles,
                          tiles_per_row=tiles_per_row),
        out_shape=tuple(out_shape),
        grid=(n_tiles + 1,),
        in_specs=in_specs,
        out_specs=tuple(out_specs),
        scratch_shapes=[pltpu.VMEM((HALO + tt, N_COMB), f32),
                        pltpu.VMEM((2, 5, tt, RWKV_WIDTH), f32)],
        compiler_params=pltpu.CompilerParams(
            dimension_semantics=("arbitrary",), vmem_limit_bytes=VMEM_LIMIT),
        name="mix_in_vres" if has_vres else "mix_in",
    )(*args)


def _scan_kernel(at_ref, rt_ref, bt_ref, kt_ref, bh_ref, kh_ref, vb_ref, bonus_ref, g_ref, egc_ref,
                 gnw_ref, gnb_ref, o_ref, z_ref, wb_ref, zy_ref, dcol_ref, y_ref, *, ts):
    j = pl.program_id(1)

    @pl.when(j == 0)
    def _():
        z_ref[...] = jnp.zeros(z_ref.shape, f32)

    c = CHUNK
    n_chunks = ts // c
    row2 = lax.broadcasted_iota(jnp.int32, (2 * c, 2 * c), 0)
    col2 = lax.broadcasted_iota(jnp.int32, (2 * c, 2 * c), 1)
    rm, cm = row2 & (c - 1), col2 & (c - 1)
    a_mask = (rm > cm) | ((row2 >= c) & (rm == cm))
    rowv = lax.broadcasted_iota(jnp.int32, (c, 2 * c), 0)
    colv = lax.broadcasted_iota(jnp.int32, (c, 2 * c), 1)
    low_half = colv < c
    heads = range(N_HEADS)
    hsl = _HEAD_SLICES

    def chunk_rows(ci):
        return pl.ds(pl.multiple_of(ci * c, c), c)

    def solve_chunks(cis):
        keys = [(n, h) for n in range(len(cis)) for h in heads]
        ws = {}
        for n, ci in enumerate(cis):
            rows = chunk_rows(ci)
            at, rt, bt, kt = at_ref[0, rows, :], rt_ref[0, rows, :], bt_ref[0, rows, :], kt_ref[0, rows, :]
            bh, kh = bh_ref[0, rows, :], kh_ref[0, rows, :]
            egc = egc_ref[0, pl.ds(pl.multiple_of(ci * SUBLANES, SUBLANES), SUBLANES), :][0:1, :]
            for h in heads:
                amat = _dot_nt(jnp.concatenate([at[:, hsl[h]], rt[:, hsl[h]]], axis=0),
                               jnp.concatenate([bt[:, hsl[h]], kt[:, hsl[h]]], axis=0))
                amat = jnp.where(a_mask, amat, 0.0)
                ws[n, h] = jnp.concatenate([amat[0:c, :], at[:, hsl[h]].astype(f32)], axis=1)
                hat = jnp.transpose(jnp.concatenate([bh[:, hsl[h]], kh[:, hsl[h]]], axis=0))
                zy_ref[ci, h] = jnp.concatenate(
                    [jnp.concatenate([hat, jnp.zeros((c, c), bf16)], axis=1),
                     jnp.concatenate([amat[c:2 * c, :].astype(bf16), rt[:, hsl[h]]], axis=1)], axis=0)
            for p in range(N_HEADS // 2):
                dcol = jnp.transpose(jnp.broadcast_to(egc[:, 2 * c * p:2 * c * (p + 1)], (c, 2 * c)))
                dcol_ref[ci, 2 * p] = dcol[0:c, :]
                dcol_ref[ci, 2 * p + 1] = dcol[c:2 * c, :]
        for lvl in range(6):
            same = low_half & ((rowv >> (lvl + 1)) == (colv >> (lvl + 1)))
            for key in keys:
                w0, w1 = ws[key][:, 0:2 * c], ws[key][:, 2 * c:3 * c]
                dmat = jnp.where(same, w0, 0.0)[:, 0:c]
                upd = _dot(dmat.astype(bf16), ws[key].astype(bf16))
                ws[key] = jnp.concatenate([jnp.where(same, 0.0, w0), w1], axis=1) + upd
        for n, h in keys:
            wb_ref[cis[n], h] = ws[n, h].astype(bf16)

    def recur(ci):
        vb = vb_ref[0, chunk_rows(ci), :]
        zero_rows = jnp.zeros((c, HEAD_SIZE), bf16)
        zs = [z_ref[h] for h in heads]
        zbs = [z.astype(bf16) for z in zs]
        vbs = [vb[:, hsl[h]] for h in heads]
        ubs = [_dot(wb_ref[ci, h], jnp.concatenate([zero_rows, vbs[h], zbs[h]], axis=0)).astype(bf16)
               for h in heads]
        for h in heads:
            zy = _dot(zy_ref[ci, h], jnp.concatenate([ubs[h], vbs[h], zbs[h]], axis=0))
            z_ref[h] = dcol_ref[ci, h] * zs[h] + zy[0:c, :]
            y_ref[h] = zy[c:2 * c, :]

    def finish(ci):
        rows = chunk_rows(ci)
        ycs = [y_ref[h] - jnp.mean(y_ref[h], axis=-1, keepdims=True) for h in heads]
        rstd = [lax.rsqrt(jnp.mean(yc * yc, axis=-1, keepdims=True) + GN_EPS) for yc in ycs]
        yn = jnp.concatenate(ycs, axis=1) * _head_bcast(rstd, low_half) * gnw_ref[...] + gnb_ref[...]
        o_ref[0, rows, :] = (yn + bonus_ref[0, rows, :]) * g_ref[0, rows, :]

    def pair_body(i, carry):
        solve_chunks([2 * i, 2 * i + 1])
        return carry

    lax.fori_loop(0, n_chunks // 2, pair_body, 0)
    if n_chunks % 2:
        solve_chunks([n_chunks - 1])

    recur(0)

    def seq_body(ci, carry):
        finish(ci - 1)
        recur(ci)
        return carry

    lax.fori_loop(1, n_chunks, seq_body, 0)
    finish(n_chunks - 1)


def _scan(at, rt, bt, kt, bh, kh, vb, bonus, g, egc, gn_w, gn_b):
    bsz, tp, _ = at.shape
    ts = TT_SCAN
    n_chunks = ts // CHUNK
    tile = pl.BlockSpec((1, ts, RWKV_WIDTH), lambda b, j: (b, j, 0))
    etile = pl.BlockSpec((1, n_chunks * SUBLANES, RWKV_WIDTH), lambda b, j: (b, j, 0))
    cspec = pl.BlockSpec((1, RWKV_WIDTH), lambda b, j: (0, 0))
    return pl.pallas_call(
        functools.partial(_scan_kernel, ts=ts),
        out_shape=jax.ShapeDtypeStruct((bsz, tp, RWKV_WIDTH), f32),
        grid=(bsz, tp // ts),
        in_specs=[tile] * 9 + [etile] + [cspec] * 2,
        out_specs=tile,
        scratch_shapes=[
            pltpu.VMEM((N_HEADS, HEAD_SIZE, HEAD_SIZE), f32),
            pltpu.VMEM((n_chunks, N_HEADS, CHUNK, 3 * CHUNK), bf16),
            pltpu.VMEM((n_chunks, N_HEADS, 2 * CHUNK, 3 * CHUNK), bf16),
            pltpu.VMEM((n_chunks, N_HEADS, HEAD_SIZE, HEAD_SIZE), f32),
            pltpu.VMEM((N_HEADS, CHUNK, HEAD_SIZE), f32),
        ],
        compiler_params=pltpu.CompilerParams(
            dimension_semantics=("arbitrary", "arbitrary"), vmem_limit_bytes=VMEM_LIMIT),
        name="rwkv_scan",
    )(at, rt, bt, kt, bh, kh, vb, bonus, g, egc, gn_w, gn_b)


def _out_ffn_kernel(h_ref, pool_ref, rw_ref, wo_ref, g2_ref, wg_ref, wu_ref, wd_ref, gf_ref,
                    o_ref, *, final):
    mixed = (_dot(pool_ref[...].astype(bf16), wo_ref[0:POOL_WIDTH, :])
             + _dot(rw_ref[...].astype(bf16), wo_ref[POOL_WIDTH:, :]))
    h1 = h_ref[...] + mixed
    hn = h1 * lax.rsqrt(jnp.mean(h1 * h1, axis=-1, keepdims=True) + RMS_EPS) * g2_ref[...]
    hb = hn.astype(bf16)
    acc = h1
    for blk in range(D_FF // FF_BLK):
        cs = slice(blk * FF_BLK, (blk + 1) * FF_BLK)
        gate = _dot(hb, wg_ref[:, cs])
        up = _dot(hb, wu_ref[:, cs])
        act = gate * _sigmoid(gate) * up
        acc = acc + _dot(act.astype(bf16), wd_ref[cs, :])
    if final:
        acc = acc * lax.rsqrt(jnp.mean(acc * acc, axis=-1, keepdims=True) + RMS_EPS) * gf_ref[...]
    o_ref[...] = acc


def _out_ffn(h, pool, rw, w_out, g2, w_gate, w_up, w_down, g_final, final):
    rows = h.shape[0]
    tm = TM_FFN
    tile = lambda width: pl.BlockSpec((tm, width), lambda i: (i, 0))
    cspec = lambda shape: pl.BlockSpec(shape, lambda i: (0, 0), pipeline_mode=pl.Buffered(1))
    return pl.pallas_call(
        functools.partial(_out_ffn_kernel, final=final),
        out_shape=jax.ShapeDtypeStruct((rows, D_MODEL), f32),
        grid=(rows // tm,),
        in_specs=[tile(D_MODEL), tile(POOL_WIDTH), tile(RWKV_WIDTH),
                  cspec((D_MODEL, D_MODEL)), cspec((1, D_MODEL)),
                  cspec((D_MODEL, D_FF)), cspec((D_MODEL, D_FF)), cspec((D_FF, D_MODEL)),
                  cspec((1, D_MODEL))],
        out_specs=tile(D_MODEL),
        compiler_params=pltpu.CompilerParams(
            dimension_semantics=("arbitrary",), vmem_limit_bytes=VMEM_LIMIT),
        name="out_ffn_final" if final else "out_ffn",
    )(h, pool, rw, w_out, g2, w_gate, w_up, w_down, g_final)


def _row(p):
    return p.reshape(1, -1).astype(f32)


def _pad_rows(w, before, total):
    return jnp.pad(w, ((before, total - before - w.shape[0]), (0, 0)))


def kernel(x, meta_tokens, ln1_g, w_in, mu_shift, pool_w, pool_scale, w0, w_lora_up, a0, a_lora_up, g_lora_up, k_k, k_a, r_k, gn_w, gn_b, w_in_vres, mu_vres, v0, v_lora_up, w_out, ln2_g, w_gate_up, w_down, final_g):
    bsz, seq, _ = x.shape
    depth = w_in.shape[0]
    t_real = N_META + seq
    tp = SEQ_PAD_TO
    assert t_real <= tp and tp % TT_MIX == 0 and tp % TT_SCAN == 0 and (bsz * tp) % TM_FFN == 0
    meta = jnp.broadcast_to(meta_tokens.astype(x.dtype)[None], (bsz, N_META, D_MODEL))
    h = jnp.concatenate([meta, x, jnp.zeros((bsz, tp - t_real, D_MODEL), x.dtype)], axis=1)

    v_first = None
    for l in range(depth):
        n_in = w_in.shape[2]
        w_cols = [w_in[l]]
        mu_cols = [jnp.zeros((POOL_WIDTH,), f32), mu_shift[l].astype(f32)]
        used = n_in
        if l > 0:
            w_cols.append(w_in_vres[l - 1])
            mu_cols.append(mu_vres[l - 1].astype(f32))
            used += MV_LORA
        w_comb = jnp.pad(jnp.concatenate(w_cols, axis=1), ((0, 0), (0, N_COMB - used))).astype(bf16)
        mu = jnp.pad(jnp.concatenate(mu_cols), (0, N_COMB - used)).reshape(1, N_COMB)
        wup = _pad_rows(w_lora_up[l], 0, LO1_W).astype(bf16)
        aup = _pad_rows(a_lora_up[l], DECAY_LORA, LO1_W).astype(bf16)
        gup = _pad_rows(g_lora_up[l], 0, LO2_W).astype(bf16)
        vres = None
        if l > 0:
            vup = _pad_rows(v_lora_up[l - 1], GATE_LORA, LO2_W).astype(bf16)
            vres = (_row(v0[l - 1]), vup, v_first)
        outs = _mix_in(
            h, _row(ln1_g[l]), w_comb, mu, pool_w[l].astype(bf16), _row(pool_scale[l]),
            _row(w0[l]), wup, _row(a0[l]), aup, gup, _row(k_k[l]), _row(k_a[l]), _row(r_k[l]), vres)
        pool_o, scan_args = outs[0], outs[1:11]
        if l == 0:
            v_first = outs[11]
        rw = _scan(*scan_args, _row(gn_w[l]), _row(gn_b[l]))
        wgu = w_gate_up[l].astype(bf16)
        flat = lambda t: t.reshape(bsz * tp, t.shape[-1])
        h = _out_ffn(flat(h), flat(pool_o), flat(rw), w_out[l].astype(bf16), _row(ln2_g[l]),
                     wgu[:, :D_FF], wgu[:, D_FF:], w_down[l].astype(bf16), _row(final_g),
                     final=(l == depth - 1)).reshape(bsz, tp, D_MODEL)
    return h[:, N_META:t_real]
```

```python
import functools

import jax
import jax.numpy as jnp
from jax import lax
from jax.experimental import pallas as pl
from jax.experimental.pallas import tpu as pltpu

f32 = jnp.float32
bf16 = jnp.bfloat16

D_MODEL = 1024
N_META = 16
POOL_WIDTH = 512
POOL_WINDOWS = (2, 4, 8, 16)
POOL_GROUP = 128
RWKV_WIDTH = 512
HEAD_SIZE = 64
N_HEADS = 8
DECAY_LORA = 64
AAA_LORA = 64
MV_LORA = 32
GATE_LORA = 160
D_FF = 2816
RMS_EPS = 1e-6
GN_EPS = 64e-5
L2_EPS = 1e-12

C_R = POOL_WIDTH
C_K = C_R + RWKV_WIDTH
C_V = C_K + RWKV_WIDTH
C_LO1 = C_V + RWKV_WIDTH
C_LO2 = C_LO1 + DECAY_LORA + AAA_LORA
LO1_W = 128
LO2_W = 256
N_COMB = C_LO2 + LO2_W

CHUNK = 64
SEQ_PAD_TO = CHUNK * 33
HALO = 16
SUBLANES = 8
TT_MIX = 3 * CHUNK
TT_SCAN = 11 * CHUNK
TM_FFN = 512
FF_BLKS = (768, 768, 768, 512)
VMEM_LIMIT = 56 * 1024 * 1024

_HEAD_SLICES = [slice(h * HEAD_SIZE, (h + 1) * HEAD_SIZE) for h in range(N_HEADS)]


def _dot(a, b):
    return jnp.dot(a, b, preferred_element_type=f32)


def _dot_nt(a, b):
    return lax.dot_general(a, b, (((1,), (1,)), ((), ())), preferred_element_type=f32)


def _sigmoid(x):
    return 1.0 / (1.0 + jnp.exp(-x))


def _head_sums(x):
    return [jnp.sum(x[:, hs], axis=-1, keepdims=True) for hs in _HEAD_SLICES]


def _head_bcast(cols, low_half):
    return jnp.concatenate(
        [jnp.where(low_half, cols[2 * p], cols[2 * p + 1]) for p in range(N_HEADS // 2)], axis=1)


def _mix_in_kernel(*refs, has_vres, tt, n_tiles, tiles_per_row):
    if has_vres:
        (h_ref, g1_ref, w_ref, mu_ref, pw_ref, ps_ref, w0_ref, wup_ref, a0_ref, aup_ref,
         gup_ref, kk_ref, ka_ref, rk_ref, v0_ref, vup_ref, vf_ref,
         pool_ref, at_ref, rt_ref, bt_ref, kt_ref, bh_ref, kh_ref, vb_ref, bonus_ref, g_ref,
         egc_ref, e_ref, stash_ref) = refs
        vout_ref = None
    else:
        (h_ref, g1_ref, w_ref, mu_ref, pw_ref, ps_ref, w0_ref, wup_ref, a0_ref, aup_ref,
         gup_ref, kk_ref, ka_ref, rk_ref,
         pool_ref, at_ref, rt_ref, bt_ref, kt_ref, bh_ref, kh_ref, vb_ref, bonus_ref, g_ref,
         egc_ref, vout_ref, e_ref, stash_ref) = refs
    s = pl.program_id(0)
    j = jnp.minimum(s, n_tiles - 1) % tiles_per_row
    c = CHUNK

    @pl.when(s == 0)
    def _():
        stash_ref[...] = jnp.zeros(stash_ref.shape, f32)

    @pl.when(j == 0)
    def _():
        e_ref[0:HALO, :] = jnp.zeros((HALO, N_COMB), f32)

    prev = stash_ref.at[(s + 1) % 2]
    r, kraw, v, lw, a_sig = prev[0], prev[1], prev[2], prev[3], prev[4]
    row = lax.broadcasted_iota(jnp.int32, (c, c), 0)
    col = lax.broadcasted_iota(jnp.int32, (c, c), 1)
    tri_incl = (row >= col).astype(bf16)
    lw_hi = lw.astype(bf16)
    lw_r1 = lw - lw_hi.astype(f32)
    lw_mid = lw_r1.astype(bf16)
    lw_lo = (lw_r1 - lw_mid.astype(f32)).astype(bf16)
    g_parts, egc_parts = [], []
    for ci in range(tt // c):
        rows = slice(ci * c, (ci + 1) * c)
        g_ci = _dot(tri_incl, lw_hi[rows]) + _dot(tri_incl, lw_mid[rows]) + _dot(tri_incl, lw_lo[rows])
        egc = jnp.exp(g_ci[c - 1:c, :])
        egc_ref[0, ci * SUBLANES:(ci + 1) * SUBLANES, :] = jnp.broadcast_to(egc, (SUBLANES, RWKV_WIDTH))
        g_parts.append(g_ci)
        egc_parts.append(jnp.broadcast_to(egc, (c, RWKV_WIDTH)))
    g = jnp.concatenate(g_parts, axis=0)
    egc_rows = jnp.concatenate(egc_parts, axis=0)
    low_half = lax.broadcasted_iota(jnp.int32, (tt, 2 * HEAD_SIZE), 1) < HEAD_SIZE
    kkx = kraw * kk_ref[...]
    inv_norm = [lax.rsqrt(jnp.maximum(ss, L2_EPS * L2_EPS)) for ss in _head_sums(kkx * kkx)]
    kk = kkx * _head_bcast(inv_norm, low_half)
    kf = kraw * (1.0 + (a_sig - 1.0) * ka_ref[...])
    eng = jnp.exp(-g)
    kt = kf * eng
    bt = kk * a_sig * eng
    rt_ref[0] = (r * jnp.exp(g)).astype(bf16)
    at_ref[0] = (-(kk * jnp.exp(g - lw))).astype(bf16)
    kt_ref[0] = kt.astype(bf16)
    bt_ref[0] = bt.astype(bf16)
    kh_ref[0] = (kt * egc_rows).astype(bf16)
    bh_ref[0] = (bt * egc_rows).astype(bf16)
    vb_ref[0] = v.astype(bf16)
    bonus_ref[0] = _head_bcast(_head_sums(r * kf * rk_ref[...]), low_half) * v

    x = h_ref[0]
    hn = x * lax.rsqrt(jnp.mean(x * x, axis=-1, keepdims=True) + RMS_EPS) * g1_ref[...]
    e_ref[HALO:HALO + tt, :] = _dot(hn.astype(bf16), w_ref[...])

    pos = j * tt + lax.broadcasted_iota(jnp.int32, (tt, 1), 0)
    for gi, win in enumerate(POOL_WINDOWS):
        c0 = gi * POOL_GROUP
        u_ext = e_ref[:, c0:c0 + POOL_GROUP]
        acc = u_ext
        d = 1
        while d < win:
            acc = acc + pltpu.roll(acc, d, axis=0)
            d *= 2
        cnt = jnp.minimum(pos + 1, win).astype(f32)
        diff = acc[HALO:, :] / cnt - u_ext[HALO:, :]
        y = _dot(diff.astype(bf16), pw_ref[gi])
        pool_ref[0, :, c0:c0 + POOL_GROUP] = y * ps_ref[:, c0:c0 + POOL_GROUP]

    def shifted(c0, c1):
        cur = e_ref[HALO:HALO + tt, c0:c1]
        before = e_ref[HALO - 1:HALO - 1 + tt, c0:c1]
        return cur + (before - cur) * mu_ref[:, c0:c1]

    nxt = stash_ref.at[s % 2]
    nxt[0] = shifted(C_R, C_K)
    nxt[1] = shifted(C_K, C_V)
    v_new = shifted(C_V, C_LO1)
    lo1 = shifted(C_LO1, C_LO2)
    lo2 = shifted(C_LO2, N_COMB)
    zw = w0_ref[...] + _dot(jnp.tanh(lo1).astype(bf16), wup_ref[...])
    nxt[3] = (-jnp.exp(f32(-0.5))) * _sigmoid(zw)
    nxt[4] = _sigmoid(a0_ref[...] + _dot(lo1.astype(bf16), aup_ref[...]))
    g_ref[0] = _dot(_sigmoid(lo2).astype(bf16), gup_ref[...])
    if has_vres:
        v_mix = _sigmoid(v0_ref[...] + _dot(lo2.astype(bf16), vup_ref[...]))
        v_new = v_new + (vf_ref[0] - v_new) * v_mix
    else:
        vout_ref[0] = v_new
    nxt[2] = v_new

    e_ref[0:HALO, :] = jnp.where(s < n_tiles - 1, e_ref[tt:tt + HALO, :], e_ref[0:HALO, :])


def _mix_in(h, g1, w_comb, mu, pool_w, pool_scale, w0, wup, a0, aup, gup, k_k, k_a, r_k, vres):
    bsz, tp, _ = h.shape
    tt = TT_MIX
    tiles_per_row = tp // tt
    n_tiles = bsz * tiles_per_row
    egc_rows = tt // CHUNK * SUBLANES
    has_vres = vres is not None

    def cur_map(s):
        t = jnp.minimum(s, n_tiles - 1)
        return (t // tiles_per_row, t % tiles_per_row, 0)

    def lag_map(s):
        t = jnp.maximum(s - 1, 0)
        return (t // tiles_per_row, t % tiles_per_row, 0)

    cur = lambda width: pl.BlockSpec((1, tt, width), cur_map)
    lag = lambda width: pl.BlockSpec((1, tt, width), lag_map)
    cspec = lambda shape: pl.BlockSpec(shape, lambda s: (0,) * len(shape), pipeline_mode=pl.Buffered(1))
    in_specs = [
        cur(D_MODEL), cspec((1, D_MODEL)), cspec((D_MODEL, N_COMB)), cspec((1, N_COMB)),
        cspec((len(POOL_WINDOWS), POOL_GROUP, POOL_GROUP)),
        cspec((1, POOL_WIDTH)), cspec((1, RWKV_WIDTH)), cspec((LO1_W, RWKV_WIDTH)),
        cspec((1, RWKV_WIDTH)), cspec((LO1_W, RWKV_WIDTH)), cspec((LO2_W, RWKV_WIDTH)),
        cspec((1, RWKV_WIDTH)), cspec((1, RWKV_WIDTH)), cspec((1, RWKV_WIDTH)),
    ]
    args = [h, g1, w_comb, mu, pool_w, pool_scale, w0, wup, a0, aup, gup, k_k, k_a, r_k]
    wide = lambda dtype: jax.ShapeDtypeStruct((bsz, tp, RWKV_WIDTH), dtype)
    out_shape = [wide(f32)] + [wide(bf16)] * 7 + [wide(f32), wide(f32),
                 jax.ShapeDtypeStruct((bsz, tp // CHUNK * SUBLANES, RWKV_WIDTH), f32)]
    out_specs = [cur(RWKV_WIDTH)] + [lag(RWKV_WIDTH)] * 8 + [
        cur(RWKV_WIDTH), pl.BlockSpec((1, egc_rows, RWKV_WIDTH), lag_map)]
    if has_vres:
        v0, vup, v_first = vres
        in_specs += [cspec((1, RWKV_WIDTH)), cspec((LO2_W, RWKV_WIDTH)), cur(RWKV_WIDTH)]
        args += [v0, vup, v_first]
    else:
        out_shape.append(wide(f32))
        out_specs.append(cur(RWKV_WIDTH))
    return pl.pallas_call(
        functools.partial(_mix_in_kernel, has_vres=has_vres, tt=tt, n_tiles=n_tiles,
                          tiles_per_row=tiles_per_row),
        out_shape=tuple(out_shape),
        grid=(n_tiles + 1,),
        in_specs=in_specs,
        out_specs=tuple(out_specs),
        scratch_shapes=[pltpu.VMEM((HALO + tt, N_COMB), f32),
                        pltpu.VMEM((2, 5, tt, RWKV_WIDTH), f32)],
        compiler_params=pltpu.CompilerParams(
            dimension_semantics=("arbitrary",), vmem_limit_bytes=VMEM_LIMIT),
        name="mix_in_vres" if has_vres else "mix_in",
    )(*args)


def _scan_kernel(at_ref, rt_ref, bt_ref, kt_ref, bh_ref, kh_ref, egc_ref,
                 vb_ref, bonus_ref, g_ref, gnw_ref, gnb_ref,
                 o_ref, z_ref, wb_ref, zy_ref, dcol_ref, y_ref, *, ts, tiles_per_row):
    s = pl.program_id(0)
    wr = s % 2
    rd = (s + 1) % 2

    @pl.when(s == 0)
    def _():
        wb_ref[...] = jnp.zeros(wb_ref.shape, bf16)
        zy_ref[...] = jnp.zeros(zy_ref.shape, bf16)
        dcol_ref[...] = jnp.zeros(dcol_ref.shape, f32)
        y_ref[...] = jnp.zeros(y_ref.shape, f32)

    @pl.when((s + tiles_per_row - 1) % tiles_per_row == 0)
    def _():
        z_ref[...] = jnp.zeros(z_ref.shape, f32)

    c = CHUNK
    n_chunks = ts // c
    row2 = lax.broadcasted_iota(jnp.int32, (2 * c, 2 * c), 0)
    col2 = lax.broadcasted_iota(jnp.int32, (2 * c, 2 * c), 1)
    rm, cm = row2 & (c - 1), col2 & (c - 1)
    a_mask = (rm > cm) | ((row2 >= c) & (rm == cm))
    rowv = lax.broadcasted_iota(jnp.int32, (c, 2 * c), 0)
    colv = lax.broadcasted_iota(jnp.int32, (c, 2 * c), 1)
    low_half = colv < c
    heads = range(N_HEADS)
    hsl = _HEAD_SLICES

    def chunk_rows(ci):
        return pl.ds(pl.multiple_of(ci * c, c), c)

    def solve_products(cis):
        ws = {}
        pair_lo = lax.broadcasted_iota(jnp.int32, (2 * c, 2 * c), 1) < c
        zeros_cc = jnp.zeros((c, c), bf16)
        for n, ci in enumerate(cis):
            rows = chunk_rows(ci)
            at, rt, bt, kt = at_ref[0, rows, :], rt_ref[0, rows, :], bt_ref[0, rows, :], kt_ref[0, rows, :]
            bh, kh = bh_ref[0, rows, :], kh_ref[0, rows, :]
            egc = egc_ref[0, pl.ds(pl.multiple_of(ci * SUBLANES, SUBLANES), SUBLANES), :][0:1, :]
            for p in range(N_HEADS // 2):
                lanes = slice(2 * c * p, 2 * c * (p + 1))
                la = jnp.concatenate([at[:, lanes], rt[:, lanes]], axis=0)
                ra = jnp.concatenate([bt[:, lanes], kt[:, lanes]], axis=0)
                zero = jnp.zeros_like(la)
                am2 = _dot_nt(jnp.concatenate([jnp.where(pair_lo, la, zero),
                                               jnp.where(pair_lo, zero, la)], axis=0), ra)
                hat2 = jnp.transpose(jnp.concatenate([bh[:, lanes], kh[:, lanes]], axis=0))
                dcol = jnp.transpose(jnp.broadcast_to(egc[:, lanes], (c, 2 * c)))
                for q in range(2):
                    h = 2 * p + q
                    amat = jnp.where(a_mask, am2[2 * c * q:2 * c * (q + 1), :], 0.0)
                    ws[n, h] = jnp.concatenate([amat[0:c, :], at[:, hsl[h]].astype(f32)], axis=1)
                    zy_ref[wr, ci, h] = jnp.concatenate(
                        [jnp.concatenate([hat2[c * q:c * (q + 1), :], zeros_cc], axis=1),
                         jnp.concatenate([amat[c:2 * c, :].astype(bf16), rt[:, hsl[h]]], axis=1)], axis=0)
                    dcol_ref[wr, ci, h] = dcol[c * q:c * (q + 1), :]
        return ws

    def solve_level(ws, lvl):
        same = low_half & ((rowv >> (lvl + 1)) == (colv >> (lvl + 1)))
        for key in ws:
            w0, w1 = ws[key][:, 0:2 * c], ws[key][:, 2 * c:3 * c]
            dmat = jnp.where(same, w0, 0.0)[:, 0:c]
            upd = _dot(dmat.astype(bf16), ws[key].astype(bf16))
            ws[key] = jnp.concatenate([jnp.where(same, 0.0, w0), w1], axis=1) + upd

    def solve_store(ws, cis):
        for (n, h), w in ws.items():
            wb_ref[wr, cis[n], h] = w.astype(bf16)

    def recur_u(ci):
        vb = vb_ref[0, chunk_rows(ci), :]
        zero_rows = jnp.zeros((c, HEAD_SIZE), bf16)
        zs = [z_ref[h] for h in heads]
        zbs = [z.astype(bf16) for z in zs]
        vbs = [vb[:, hsl[h]] for h in heads]
        ubs = [_dot(wb_ref[rd, ci, h], jnp.concatenate([zero_rows, vbs[h], zbs[h]], axis=0)).astype(bf16)
               for h in heads]
        return zs, zbs, vbs, ubs

    def recur_zy(ci, ctx):
        zs, zbs, vbs, ubs = ctx
        for h in heads:
            zy = _dot(zy_ref[rd, ci, h], jnp.concatenate([ubs[h], vbs[h], zbs[h]], axis=0))
            z_ref[h] = dcol_ref[rd, ci, h] * zs[h] + zy[0:c, :]
            y_ref[h] = zy[c:2 * c, :]
        rows = chunk_rows(ci)
        ycs = [y_ref[h] - jnp.mean(y_ref[h], axis=-1, keepdims=True) for h in heads]
        rstd = [lax.rsqrt(jnp.mean(yc * yc, axis=-1, keepdims=True) + GN_EPS) for yc in ycs]
        yn = jnp.concatenate(ycs, axis=1) * _head_bcast(rstd, low_half) * gnw_ref[...] + gnb_ref[...]
        o_ref[0, rows, :] = (yn + bonus_ref[0, rows, :]) * g_ref[0, rows, :]

    def pair_body(i, carry):
        cis = [2 * i, 2 * i + 1]
        ws = solve_products(cis)
        ctx = recur_u(cis[0])
        solve_level(ws, 0)
        recur_zy(cis[0], ctx)
        solve_level(ws, 1)
        ctx = recur_u(cis[1])
        solve_level(ws, 2)
        recur_zy(cis[1], ctx)
        for lvl in range(3, 6):
            solve_level(ws, lvl)
        solve_store(ws, cis)
        return carry

    lax.fori_loop(0, n_chunks // 2, pair_body, 0)
    if n_chunks % 2:
        last = [n_chunks - 1]
        ws = solve_products(last)
        ctx = recur_u(last[0])
        solve_level(ws, 0)
        recur_zy(last[0], ctx)
        for lvl in range(1, 6):
            solve_level(ws, lvl)
        solve_store(ws, last)


def _scan(at, rt, bt, kt, bh, kh, vb, bonus, g, egc, gn_w, gn_b):
    bsz, tp, _ = at.shape
    ts = TT_SCAN
    n_chunks = ts // CHUNK
    tiles_per_row = tp // ts
    n_tiles = bsz * tiles_per_row

    def cur_map(s):
        t = jnp.minimum(s, n_tiles - 1)
        return (t // tiles_per_row, t % tiles_per_row, 0)

    def lag_map(s):
        t = jnp.maximum(s - 1, 0)
        return (t // tiles_per_row, t % tiles_per_row, 0)

    cur = pl.BlockSpec((1, ts, RWKV_WIDTH), cur_map)
    lag = pl.BlockSpec((1, ts, RWKV_WIDTH), lag_map)
    ecur = pl.BlockSpec((1, n_chunks * SUBLANES, RWKV_WIDTH), cur_map)
    cspec = pl.BlockSpec((1, RWKV_WIDTH), lambda s: (0, 0))
    return pl.pallas_call(
        functools.partial(_scan_kernel, ts=ts, tiles_per_row=tiles_per_row),
        out_shape=jax.ShapeDtypeStruct((bsz, tp, RWKV_WIDTH), f32),
        grid=(n_tiles + 1,),
        in_specs=[cur] * 6 + [ecur] + [lag] * 3 + [cspec] * 2,
        out_specs=lag,
        scratch_shapes=[
            pltpu.VMEM((N_HEADS, HEAD_SIZE, HEAD_SIZE), f32),
            pltpu.VMEM((2, n_chunks, N_HEADS, CHUNK, 3 * CHUNK), bf16),
            pltpu.VMEM((2, n_chunks, N_HEADS, 2 * CHUNK, 3 * CHUNK), bf16),
            pltpu.VMEM((2, n_chunks, N_HEADS, HEAD_SIZE, HEAD_SIZE), f32),
            pltpu.VMEM((N_HEADS, CHUNK, HEAD_SIZE), f32),
        ],
        compiler_params=pltpu.CompilerParams(
            dimension_semantics=("arbitrary",), vmem_limit_bytes=VMEM_LIMIT),
        name="rwkv_scan",
    )(at, rt, bt, kt, bh, kh, egc, vb, bonus, g, gn_w, gn_b)


def _out_ffn_kernel(h_ref, pool_ref, rw_ref, wo_ref, g2_ref, wg_ref, wu_ref, wd_ref, gf_ref,
                    o_ref, *, final):
    mixed = (_dot(pool_ref[...].astype(bf16), wo_ref[0:POOL_WIDTH, :])
             + _dot(rw_ref[...].astype(bf16), wo_ref[POOL_WIDTH:, :]))
    h1 = h_ref[...] + mixed
    hn = h1 * lax.rsqrt(jnp.mean(h1 * h1, axis=-1, keepdims=True) + RMS_EPS) * g2_ref[...]
    hb = hn.astype(bf16)
    acc = h1
    start = 0
    for width in FF_BLKS:
        cs = slice(start, start + width)
        start += width
        gate = _dot(hb, wg_ref[:, cs])
        up = _dot(hb, wu_ref[:, cs])
        act = gate * _sigmoid(gate) * up
        acc = acc + _dot(act.astype(bf16), wd_ref[cs, :])
    if final:
        acc = acc * lax.rsqrt(jnp.mean(acc * acc, axis=-1, keepdims=True) + RMS_EPS) * gf_ref[...]
    o_ref[...] = acc


def _out_ffn(h, pool, rw, w_out, g2, w_gate, w_up, w_down, g_final, final):
    rows = h.shape[0]
    tm = TM_FFN
    assert sum(FF_BLKS) == D_FF
    tile = lambda width: pl.BlockSpec((tm, width), lambda i: (i, 0))
    cspec = lambda shape: pl.BlockSpec(shape, lambda i: (0, 0), pipeline_mode=pl.Buffered(1))
    return pl.pallas_call(
        functools.partial(_out_ffn_kernel, final=final),
        out_shape=jax.ShapeDtypeStruct((rows, D_MODEL), f32),
        grid=(rows // tm,),
        in_specs=[tile(D_MODEL), tile(POOL_WIDTH), tile(RWKV_WIDTH),
                  cspec((D_MODEL, D_MODEL)), cspec((1, D_MODEL)),
                  cspec((D_MODEL, D_FF)), cspec((D_MODEL, D_FF)), cspec((D_FF, D_MODEL)),
                  cspec((1, D_MODEL))],
        out_specs=tile(D_MODEL),
        compiler_params=pltpu.CompilerParams(
            dimension_semantics=("arbitrary",), vmem_limit_bytes=VMEM_LIMIT),
        name="out_ffn_final" if final else "out_ffn",
    )(h, pool, rw, w_out, g2, w_gate, w_up, w_down, g_final)


def _row(p):
    return p.reshape(1, -1).astype(f32)


def _pad_rows(w, before, total):
    return jnp.pad(w, ((before, total - before - w.shape[0]), (0, 0)))


def kernel(x, meta_tokens, ln1_g, w_in, mu_shift, pool_w, pool_scale, w0, w_lora_up, a0, a_lora_up, g_lora_up, k_k, k_a, r_k, gn_w, gn_b, w_in_vres, mu_vres, v0, v_lora_up, w_out, ln2_g, w_gate_up, w_down, final_g):
    bsz, seq, _ = x.shape
    depth = w_in.shape[0]
    t_real = N_META + seq
    tp = SEQ_PAD_TO
    assert t_real <= tp and tp % TT_MIX == 0 and tp % TT_SCAN == 0 and (bsz * tp) % TM_FFN == 0
    meta = jnp.broadcast_to(meta_tokens.astype(x.dtype)[None], (bsz, N_META, D_MODEL))
    h = jnp.concatenate([meta, x, jnp.zeros((bsz, tp - t_real, D_MODEL), x.dtype)], axis=1)

    v_first = None
    for l in range(depth):
        n_in = w_in.shape[2]
        w_cols = [w_in[l]]
        mu_cols = [jnp.zeros((POOL_WIDTH,), f32), mu_shift[l].astype(f32)]
        used = n_in
        if l > 0:
            w_cols.append(w_in_vres[l - 1])
            mu_cols.append(mu_vres[l - 1].astype(f32))
            used += MV_LORA
        w_comb = jnp.pad(jnp.concatenate(w_cols, axis=1), ((0, 0), (0, N_COMB - used))).astype(bf16)
        mu = jnp.pad(jnp.concatenate(mu_cols), (0, N_COMB - used)).reshape(1, N_COMB)
        wup = _pad_rows(w_lora_up[l], 0, LO1_W).astype(bf16)
        aup = _pad_rows(a_lora_up[l], DECAY_LORA, LO1_W).astype(bf16)
        gup = _pad_rows(g_lora_up[l], 0, LO2_W).astype(bf16)
        vres = None
        if l > 0:
            vup = _pad_rows(v_lora_up[l - 1], GATE_LORA, LO2_W).astype(bf16)
            vres = (_row(v0[l - 1]), vup, v_first)
        outs = _mix_in(
            h, _row(ln1_g[l]), w_comb, mu, pool_w[l].astype(bf16), _row(pool_scale[l]),
            _row(w0[l]), wup, _row(a0[l]), aup, gup, _row(k_k[l]), _row(k_a[l]), _row(r_k[l]), vres)
        pool_o, scan_args = outs[0], outs[1:11]
        if l == 0:
            v_first = outs[11]
        rw = _scan(*scan_args, _row(gn_w[l]), _row(gn_b[l]))
        wgu = w_gate_up[l].astype(bf16)
        flat = lambda t: t.reshape(bsz * tp, t.shape[-1])
        h = _out_ffn(flat(h), flat(pool_o), flat(rw), w_out[l].astype(bf16), _row(ln2_g[l]),
                     wgu[:, :D_FF], wgu[:, D_FF:], w_down[l].astype(bf16), _row(final_g),
                     final=(l == depth - 1)).reshape(bsz, tp, D_MODEL)
    return h[:, N_META:t_real]
```

```python
import functools

import jax
import jax.numpy as jnp
from jax import lax
from jax.experimental import pallas as pl
from jax.experimental.pallas import tpu as pltpu

f32 = jnp.float32
bf16 = jnp.bfloat16

D_MODEL = 1024
N_META = 16
POOL_WIDTH = 512
POOL_WINDOWS = (2, 4, 8, 16)
POOL_GROUP = 128
RWKV_WIDTH = 512
HEAD_SIZE = 64
N_HEADS = 8
DECAY_LORA = 64
AAA_LORA = 64
MV_LORA = 32
GATE_LORA = 160
D_FF = 2816
RMS_EPS = 1e-6
GN_EPS = 64e-5
L2_EPS = 1e-12

C_R = POOL_WIDTH
C_K = C_R + RWKV_WIDTH
C_V = C_K + RWKV_WIDTH
C_LO1 = C_V + RWKV_WIDTH
C_LO2 = C_LO1 + DECAY_LORA + AAA_LORA
LO1_W = 128
LO2_W = 256
N_COMB = C_LO2 + LO2_W

CHUNK = 64
SEQ_PAD_TO = CHUNK * 33
HALO = 16
SUBLANES = 8
TT_MIX = 3 * CHUNK
TT_SCAN = 11 * CHUNK
TM_FFN = 512
FF_BLKS = (768, 768, 768, 512)
VMEM_LIMIT = 56 * 1024 * 1024

_HEAD_SLICES = [slice(h * HEAD_SIZE, (h + 1) * HEAD_SIZE) for h in range(N_HEADS)]


def _dot(a, b):
    return jnp.dot(a, b, preferred_element_type=f32)


def _dot_nt(a, b):
    return lax.dot_general(a, b, (((1,), (1,)), ((), ())), preferred_element_type=f32)


def _sigmoid(x):
    return 1.0 / (1.0 + jnp.exp(-x))


def _head_sums(x):
    return [jnp.sum(x[:, hs], axis=-1, keepdims=True) for hs in _HEAD_SLICES]


def _head_bcast(cols, low_half):
    return jnp.concatenate(
        [jnp.where(low_half, cols[2 * p], cols[2 * p + 1]) for p in range(N_HEADS // 2)], axis=1)


def _mix_in_kernel(*refs, has_vres, tt, n_tiles, tiles_per_row):
    if has_vres:
        (h_ref, g1_ref, w_ref, mu_ref, pw_ref, ps_ref, w0_ref, wup_ref, a0_ref, aup_ref,
         gup_ref, kk_ref, ka_ref, rk_ref, v0_ref, vup_ref, vf_ref,
         pool_ref, at_ref, rt_ref, bt_ref, kt_ref, bh_ref, kh_ref, vb_ref, bonus_ref, g_ref,
         egc_ref, e_ref, stash_ref) = refs
        vout_ref = None
    else:
        (h_ref, g1_ref, w_ref, mu_ref, pw_ref, ps_ref, w0_ref, wup_ref, a0_ref, aup_ref,
         gup_ref, kk_ref, ka_ref, rk_ref,
         pool_ref, at_ref, rt_ref, bt_ref, kt_ref, bh_ref, kh_ref, vb_ref, bonus_ref, g_ref,
         egc_ref, vout_ref, e_ref, stash_ref) = refs
    s = pl.program_id(0)
    j = jnp.minimum(s, n_tiles - 1) % tiles_per_row
    c = CHUNK

    @pl.when(s == 0)
    def _():
        stash_ref[...] = jnp.zeros(stash_ref.shape, f32)

    @pl.when(j == 0)
    def _():
        e_ref[0:HALO, :] = jnp.zeros((HALO, N_COMB), f32)

    prev = stash_ref.at[(s + 1) % 2]
    r, kraw, v, lw, a_sig = prev[0], prev[1], prev[2], prev[3], prev[4]
    row = lax.broadcasted_iota(jnp.int32, (c, c), 0)
    col = lax.broadcasted_iota(jnp.int32, (c, c), 1)
    tri_incl = (row >= col).astype(bf16)
    lw_hi = lw.astype(bf16)
    lw_r1 = lw - lw_hi.astype(f32)
    lw_mid = lw_r1.astype(bf16)
    lw_lo = (lw_r1 - lw_mid.astype(f32)).astype(bf16)
    g_parts, egc_parts = [], []
    for ci in range(tt // c):
        rows = slice(ci * c, (ci + 1) * c)
        g_ci = _dot(tri_incl, lw_hi[rows]) + _dot(tri_incl, lw_mid[rows]) + _dot(tri_incl, lw_lo[rows])
        egc = jnp.exp(g_ci[c - 1:c, :])
        egc_ref[0, ci * SUBLANES:(ci + 1) * SUBLANES, :] = jnp.broadcast_to(egc, (SUBLANES, RWKV_WIDTH))
        g_parts.append(g_ci)
        egc_parts.append(jnp.broadcast_to(egc, (c, RWKV_WIDTH)))
    g = jnp.concatenate(g_parts, axis=0)
    egc_rows = jnp.concatenate(egc_parts, axis=0)
    low_half = lax.broadcasted_iota(jnp.int32, (tt, 2 * HEAD_SIZE), 1) < HEAD_SIZE
    kkx = kraw * kk_ref[...]
    inv_norm = [lax.rsqrt(jnp.maximum(ss, L2_EPS * L2_EPS)) for ss in _head_sums(kkx * kkx)]
    kk = kkx * _head_bcast(inv_norm, low_half)
    kf = kraw * (1.0 + (a_sig - 1.0) * ka_ref[...])
    eng = jnp.exp(-g)
    kt = kf * eng
    bt = kk * a_sig * eng
    rt_ref[0] = (r * jnp.exp(g)).astype(bf16)
    at_ref[0] = (-(kk * jnp.exp(g - lw))).astype(bf16)
    kt_ref[0] = kt.astype(bf16)
    bt_ref[0] = bt.astype(bf16)
    kh_ref[0] = (kt * egc_rows).astype(bf16)
    bh_ref[0] = (bt * egc_rows).astype(bf16)
    vb_ref[0] = v.astype(bf16)
    bonus_ref[0] = _head_bcast(_head_sums(r * kf * rk_ref[...]), low_half) * v

    x = h_ref[0]
    hn = x * lax.rsqrt(jnp.mean(x * x, axis=-1, keepdims=True) + RMS_EPS) * g1_ref[...]
    e_ref[HALO:HALO + tt, :] = _dot(hn.astype(bf16), w_ref[...])

    pos = j * tt + lax.broadcasted_iota(jnp.int32, (tt, 1), 0)
    for gi, win in enumerate(POOL_WINDOWS):
        c0 = gi * POOL_GROUP
        u_ext = e_ref[:, c0:c0 + POOL_GROUP]
        acc = u_ext
        d = 1
        while d < win:
            acc = acc + pltpu.roll(acc, d, axis=0)
            d *= 2
        cnt = jnp.minimum(pos + 1, win).astype(f32)
        diff = acc[HALO:, :] / cnt - u_ext[HALO:, :]
        y = _dot(diff.astype(bf16), pw_ref[gi])
        pool_ref[0, :, c0:c0 + POOL_GROUP] = y * ps_ref[:, c0:c0 + POOL_GROUP]

    def shifted(c0, c1):
        cur = e_ref[HALO:HALO + tt, c0:c1]
        before = e_ref[HALO - 1:HALO - 1 + tt, c0:c1]
        return cur + (before - cur) * mu_ref[:, c0:c1]

    nxt = stash_ref.at[s % 2]
    nxt[0] = shifted(C_R, C_K)
    nxt[1] = shifted(C_K, C_V)
    v_new = shifted(C_V, C_LO1)
    lo1 = shifted(C_LO1, C_LO2)
    lo2 = shifted(C_LO2, N_COMB)
    zw = w0_ref[...] + _dot(jnp.tanh(lo1).astype(bf16), wup_ref[...])
    nxt[3] = (-jnp.exp(f32(-0.5))) * _sigmoid(zw)
    nxt[4] = _sigmoid(a0_ref[...] + _dot(lo1.astype(bf16), aup_ref[...]))
    g_ref[0] = _dot(_sigmoid(lo2).astype(bf16), gup_ref[...])
    if has_vres:
        v_mix = _sigmoid(v0_ref[...] + _dot(lo2.astype(bf16), vup_ref[...]))
        v_new = v_new + (vf_ref[0] - v_new) * v_mix
    else:
        vout_ref[0] = v_new
    nxt[2] = v_new

    e_ref[0:HALO, :] = jnp.where(s < n_tiles - 1, e_ref[tt:tt + HALO, :], e_ref[0:HALO, :])


def _mix_in(h, g1, w_comb, mu, pool_w, pool_scale, w0, wup, a0, aup, gup, k_k, k_a, r_k, vres):
    bsz, tp, _ = h.shape
    tt = TT_MIX
    tiles_per_row = tp // tt
    n_tiles = bsz * tiles_per_row
    egc_rows = tt // CHUNK * SUBLANES
    has_vres = vres is not None

    def cur_map(s):
        t = jnp.minimum(s, n_tiles - 1)
        return (t // tiles_per_row, t % tiles_per_row, 0)

    def lag_map(s):
        t = jnp.maximum(s - 1, 0)
        return (t // tiles_per_row, t % tiles_per_row, 0)

    cur = lambda width: pl.BlockSpec((1, tt, width), cur_map)
    lag = lambda width: pl.BlockSpec((1, tt, width), lag_map)
    cspec = lambda shape: pl.BlockSpec(shape, lambda s: (0,) * len(shape), pipeline_mode=pl.Buffered(1))
    in_specs = [
        cur(D_MODEL), cspec((1, D_MODEL)), cspec((D_MODEL, N_COMB)), cspec((1, N_COMB)),
        cspec((len(POOL_WINDOWS), POOL_GROUP, POOL_GROUP)),
        cspec((1, POOL_WIDTH)), cspec((1, RWKV_WIDTH)), cspec((LO1_W, RWKV_WIDTH)),
        cspec((1, RWKV_WIDTH)), cspec((LO1_W, RWKV_WIDTH)), cspec((LO2_W, RWKV_WIDTH)),
        cspec((1, RWKV_WIDTH)), cspec((1, RWKV_WIDTH)), cspec((1, RWKV_WIDTH)),
    ]
    args = [h, g1, w_comb, mu, pool_w, pool_scale, w0, wup, a0, aup, gup, k_k, k_a, r_k]
    wide = lambda dtype: jax.ShapeDtypeStruct((bsz, tp, RWKV_WIDTH), dtype)
    out_shape = [wide(f32)] + [wide(bf16)] * 7 + [wide(f32), wide(f32),
                 jax.ShapeDtypeStruct((bsz, tp // CHUNK * SUBLANES, RWKV_WIDTH), f32)]
    out_specs = [cur(RWKV_WIDTH)] + [lag(RWKV_WIDTH)] * 8 + [
        cur(RWKV_WIDTH), pl.BlockSpec((1, egc_rows, RWKV_WIDTH), lag_map)]
    if has_vres:
        v0, vup, v_first = vres
        in_specs += [cspec((1, RWKV_WIDTH)), cspec((LO2_W, RWKV_WIDTH)), cur(RWKV_WIDTH)]
        args += [v0, vup, v_first]
    else:
        out_shape.append(wide(f32))
        out_specs.append(cur(RWKV_WIDTH))
    return pl.pallas_call(
        functools.partial(_mix_in_kernel, has_vres=has_vres, tt=tt, n_tiles=n_tiles,
                          tiles_per_row=tiles_per_row),
        out_shape=tuple(out_shape),
        grid=(n_tiles + 1,),
        in_specs=in_specs,
        out_specs=tuple(out_specs),
        scratch_shapes=[pltpu.VMEM((HALO + tt, N_COMB), f32),
                        pltpu.VMEM((2, 5, tt, RWKV_WIDTH), f32)],
        compiler_params=pltpu.CompilerParams(
            dimension_semantics=("arbitrary",), vmem_limit_bytes=VMEM_LIMIT),
        name="mix_in_vres" if has_vres else "mix_in",
    )(*args)


def _scan_kernel(at_ref, rt_ref, bt_ref, kt_ref, bh_ref, kh_ref, egc_ref,
                 vb_ref, bonus_ref, g_ref, gnw_ref, gnb_ref,
                 o_ref, z_ref, wb_ref, zy_ref, dcol_ref, *, ts, tiles_per_row):
    s = pl.program_id(0)
    wr = s % 2
    rd = (s + 1) % 2

    @pl.when(s == 0)
    def _():
        wb_ref[...] = jnp.zeros(wb_ref.shape, bf16)
        zy_ref[...] = jnp.zeros(zy_ref.shape, bf16)
        dcol_ref[...] = jnp.zeros(dcol_ref.shape, f32)

    @pl.when((s + tiles_per_row - 1) % tiles_per_row == 0)
    def _():
        z_ref[...] = jnp.zeros(z_ref.shape, f32)

    c = CHUNK
    n_chunks = ts // c
    row2 = lax.broadcasted_iota(jnp.int32, (2 * c, 2 * c), 0)
    col2 = lax.broadcasted_iota(jnp.int32, (2 * c, 2 * c), 1)
    rm, cm = row2 & (c - 1), col2 & (c - 1)
    a_mask = (rm > cm) | ((row2 >= c) & (rm == cm))
    rowv = lax.broadcasted_iota(jnp.int32, (c, 2 * c), 0)
    colv = lax.broadcasted_iota(jnp.int32, (c, 2 * c), 1)
    low_half = colv < c
    heads = range(N_HEADS)
    hsl = _HEAD_SLICES

    def chunk_rows(ci):
        return pl.ds(pl.multiple_of(ci * c, c), c)

    def solve_products(cis):
        ws, xs = {}, {}
        pair_lo = lax.broadcasted_iota(jnp.int32, (2 * c, 2 * c), 1) < c
        zeros_cc = jnp.zeros((c, c), bf16)
        eye_hi = (colv == rowv + c).astype(f32)
        for n, ci in enumerate(cis):
            rows = chunk_rows(ci)
            at, rt, bt, kt = at_ref[0, rows, :], rt_ref[0, rows, :], bt_ref[0, rows, :], kt_ref[0, rows, :]
            bh, kh = bh_ref[0, rows, :], kh_ref[0, rows, :]
            egc = egc_ref[0, pl.ds(pl.multiple_of(ci * SUBLANES, SUBLANES), SUBLANES), :][0:1, :]
            for p in range(N_HEADS // 2):
                lanes = slice(2 * c * p, 2 * c * (p + 1))
                la = jnp.concatenate([at[:, lanes], rt[:, lanes]], axis=0)
                ra = jnp.concatenate([bt[:, lanes], kt[:, lanes]], axis=0)
                zero = jnp.zeros_like(la)
                am2 = _dot_nt(jnp.concatenate([jnp.where(pair_lo, la, zero),
                                               jnp.where(pair_lo, zero, la)], axis=0), ra)
                hat2 = jnp.transpose(jnp.concatenate([bh[:, lanes], kh[:, lanes]], axis=0))
                dcol = jnp.transpose(jnp.broadcast_to(egc[:, lanes], (c, 2 * c)))
                at_pair = at[:, lanes].astype(f32)
                at_low = [at_pair, pltpu.roll(at_pair, c, axis=1)]
                for q in range(2):
                    h = 2 * p + q
                    amat = jnp.where(a_mask, am2[2 * c * q:2 * c * (q + 1), :], 0.0)
                    top = amat[0:c, :]
                    ws[n, h] = jnp.where(low_half, top, eye_hi)
                    xs[n, h] = jnp.where(low_half, at_low[q], top).astype(bf16)
                    zy_ref[wr, ci, h] = jnp.concatenate(
                        [jnp.concatenate([hat2[c * q:c * (q + 1), :], zeros_cc], axis=1),
                         jnp.concatenate([amat[c:2 * c, :].astype(bf16), rt[:, hsl[h]]], axis=1)], axis=0)
                    dcol_ref[wr, ci, h] = dcol[c * q:c * (q + 1), :]
        return ws, xs

    def solve_level(ws, lvl):
        same = low_half & ((rowv >> (lvl + 1)) == (colv >> (lvl + 1)))
        for key in ws:
            w = ws[key]
            dmat = jnp.where(same, w, 0.0)[:, 0:c]
            ws[key] = jnp.where(same, 0.0, w) + _dot(dmat.astype(bf16), w.astype(bf16))

    def solve_store(ws, xs, cis):
        zeros_rows = jnp.zeros((c, 2 * c), bf16)
        for (n, h), w in ws.items():
            wb_ref[wr, cis[n], h] = _dot(
                w.astype(bf16), jnp.concatenate([zeros_rows, xs[n, h]], axis=0)).astype(bf16)

    def recur_u(ci):
        vb = vb_ref[0, chunk_rows(ci), :]
        zs = [z_ref[h] for h in heads]
        zbs = [z.astype(bf16) for z in zs]
        vbs = [vb[:, hsl[h]] for h in heads]
        ubs = [_dot(wb_ref[rd, ci, h], jnp.concatenate([zbs[h], vbs[h]], axis=0)).astype(bf16)
               for h in heads]
        return zs, zbs, vbs, ubs

    def recur_zy(ci, ctx):
        zs, zbs, vbs, ubs = ctx
        ys = []
        for h in heads:
            zy = _dot(zy_ref[rd, ci, h], jnp.concatenate([ubs[h], vbs[h], zbs[h]], axis=0))
            z_ref[h] = dcol_ref[rd, ci, h] * zs[h] + zy[0:c, :]
            ys.append(zy[c:2 * c, :])
        rows = chunk_rows(ci)
        ycs = [y - jnp.mean(y, axis=-1, keepdims=True) for y in ys]
        rstd = [lax.rsqrt(jnp.mean(yc * yc, axis=-1, keepdims=True) + GN_EPS) for yc in ycs]
        yn = jnp.concatenate(ycs, axis=1) * _head_bcast(rstd, low_half) * gnw_ref[...] + gnb_ref[...]
        o_ref[0, rows, :] = (yn + bonus_ref[0, rows, :]) * g_ref[0, rows, :]

    def group(cis):
        pending = []
        for ci in cis:
            pending += [("u", ci), ("zy", ci)]
        ctx = [None]

        def recur_step():
            if pending:
                kind, ci = pending.pop(0)
                if kind == "u":
                    ctx[0] = recur_u(ci)
                else:
                    recur_zy(ci, ctx[0])

        recur_step()
        ws, xs = solve_products(cis)
        recur_step()
        for lvl in range(6):
            solve_level(ws, lvl)
            recur_step()
        assert not pending
        solve_store(ws, xs, cis)

    n_pairs = (-n_chunks) % 3
    n_triples = (n_chunks - 2 * n_pairs) // 3
    assert n_triples >= 0

    def triple_body(i, carry):
        group([3 * i, 3 * i + 1, 3 * i + 2])
        return carry

    lax.fori_loop(0, n_triples, triple_body, 0)
    for k in range(n_pairs):
        group([3 * n_triples + 2 * k, 3 * n_triples + 2 * k + 1])


def _scan(at, rt, bt, kt, bh, kh, vb, bonus, g, egc, gn_w, gn_b):
    bsz, tp, _ = at.shape
    ts = TT_SCAN
    n_chunks = ts // CHUNK
    tiles_per_row = tp // ts
    n_tiles = bsz * tiles_per_row

    def cur_map(s):
        t = jnp.minimum(s, n_tiles - 1)
        return (t // tiles_per_row, t % tiles_per_row, 0)

    def lag_map(s):
        t = jnp.maximum(s - 1, 0)
        return (t // tiles_per_row, t % tiles_per_row, 0)

    cur = pl.BlockSpec((1, ts, RWKV_WIDTH), cur_map)
    lag = pl.BlockSpec((1, ts, RWKV_WIDTH), lag_map)
    ecur = pl.BlockSpec((1, n_chunks * SUBLANES, RWKV_WIDTH), cur_map)
    cspec = pl.BlockSpec((1, RWKV_WIDTH), lambda s: (0, 0))
    return pl.pallas_call(
        functools.partial(_scan_kernel, ts=ts, tiles_per_row=tiles_per_row),
        out_shape=jax.ShapeDtypeStruct((bsz, tp, RWKV_WIDTH), f32),
        grid=(n_tiles + 1,),
        in_specs=[cur] * 6 + [ecur] + [lag] * 3 + [cspec] * 2,
        out_specs=lag,
        scratch_shapes=[
            pltpu.VMEM((N_HEADS, HEAD_SIZE, HEAD_SIZE), f32),
            pltpu.VMEM((2, n_chunks, N_HEADS, CHUNK, 2 * CHUNK), bf16),
            pltpu.VMEM((2, n_chunks, N_HEADS, 2 * CHUNK, 3 * CHUNK), bf16),
            pltpu.VMEM((2, n_chunks, N_HEADS, HEAD_SIZE, HEAD_SIZE), f32),
        ],
        compiler_params=pltpu.CompilerParams(
            dimension_semantics=("arbitrary",), vmem_limit_bytes=VMEM_LIMIT),
        name="rwkv_scan",
    )(at, rt, bt, kt, bh, kh, egc, vb, bonus, g, gn_w, gn_b)


def _out_ffn_kernel(h_ref, pool_ref, rw_ref, wo_ref, g2_ref, wg_ref, wu_ref, wd_ref, gf_ref,
                    o_ref, *, final):
    load = (lambda ref: ref[0]) if final else (lambda ref: ref[...])
    mixed = (_dot(load(pool_ref).astype(bf16), wo_ref[0:POOL_WIDTH, :])
             + _dot(load(rw_ref).astype(bf16), wo_ref[POOL_WIDTH:, :]))
    h1 = load(h_ref) + mixed
    hn = h1 * lax.rsqrt(jnp.mean(h1 * h1, axis=-1, keepdims=True) + RMS_EPS) * g2_ref[...]
    hb = hn.astype(bf16)
    acc = h1
    start = 0
    for width in FF_BLKS:
        cs = slice(start, start + width)
        start += width
        gate = _dot(hb, wg_ref[:, cs])
        up = _dot(hb, wu_ref[:, cs])
        act = gate * _sigmoid(gate) * up
        acc = acc + _dot(act.astype(bf16), wd_ref[cs, :])
    if final:
        o_ref[0] = acc * lax.rsqrt(jnp.mean(acc * acc, axis=-1, keepdims=True) + RMS_EPS) * gf_ref[...]
    else:
        o_ref[...] = acc


def _out_ffn(h, pool, rw, w_out, g2, w_gate_up, w_down, g_final, final, seq):
    bsz, tp, _ = h.shape
    tm = TM_FFN
    assert sum(FF_BLKS) == D_FF
    cspec = lambda shape: pl.BlockSpec(shape, lambda *_: (0, 0), pipeline_mode=pl.Buffered(1))
    weights = [cspec((D_MODEL, D_MODEL)), cspec((1, D_MODEL)),
               pl.BlockSpec((D_MODEL, D_FF), lambda *_: (0, 0), pipeline_mode=pl.Buffered(1)),
               pl.BlockSpec((D_MODEL, D_FF), lambda *_: (0, 1), pipeline_mode=pl.Buffered(1)),
               cspec((D_FF, D_MODEL)), cspec((1, D_MODEL))]
    if final:
        assert seq % tm == 0 and N_META % SUBLANES == 0
        win = lambda width: pl.BlockSpec(
            (pl.Element(1), pl.Element(tm), pl.Element(width)),
            lambda b, i: (b, pl.multiple_of(N_META + i * tm, SUBLANES), 0))
        args = (h, pool, rw)
        grid = (bsz, seq // tm)
        out_shape = jax.ShapeDtypeStruct((bsz, seq, D_MODEL), f32)
        out_spec = pl.BlockSpec((1, tm, D_MODEL), lambda b, i: (b, i, 0))
        sem = ("arbitrary", "arbitrary")
    else:
        rows = bsz * tp
        win = lambda width: pl.BlockSpec((tm, width), lambda i: (i, 0))
        args = tuple(t.reshape(rows, t.shape[-1]) for t in (h, pool, rw))
        grid = (rows // tm,)
        out_shape = jax.ShapeDtypeStruct((rows, D_MODEL), f32)
        out_spec = win(D_MODEL)
        sem = ("arbitrary",)
    out = pl.pallas_call(
        functools.partial(_out_ffn_kernel, final=final),
        out_shape=out_shape,
        grid=grid,
        in_specs=[win(D_MODEL), win(POOL_WIDTH), win(RWKV_WIDTH)] + weights,
        out_specs=out_spec,
        compiler_params=pltpu.CompilerParams(dimension_semantics=sem, vmem_limit_bytes=VMEM_LIMIT),
        name="out_ffn_final" if final else "out_ffn",
    )(*args, w_out, g2, w_gate_up, w_gate_up, w_down, g_final)
    return out if final else out.reshape(bsz, tp, D_MODEL)


def _row(p):
    return p.reshape(1, -1).astype(f32)


def _pad_rows(w, before, total):
    return jnp.pad(w, ((before, total - before - w.shape[0]), (0, 0)))


def kernel(x, meta_tokens, ln1_g, w_in, mu_shift, pool_w, pool_scale, w0, w_lora_up, a0, a_lora_up, g_lora_up, k_k, k_a, r_k, gn_w, gn_b, w_in_vres, mu_vres, v0, v_lora_up, w_out, ln2_g, w_gate_up, w_down, final_g):
    bsz, seq, _ = x.shape
    depth = w_in.shape[0]
    t_real = N_META + seq
    tp = SEQ_PAD_TO
    assert t_real <= tp and tp % TT_MIX == 0 and tp % TT_SCAN == 0 and (bsz * tp) % TM_FFN == 0
    meta = jnp.broadcast_to(meta_tokens.astype(x.dtype)[None], (bsz, N_META, D_MODEL))
    h = jnp.concatenate([meta, x, jnp.zeros((bsz, tp - t_real, D_MODEL), x.dtype)], axis=1)

    v_first = None
    for l in range(depth):
        n_in = w_in.shape[2]
        w_cols = [w_in[l]]
        mu_cols = [jnp.zeros((POOL_WIDTH,), f32), mu_shift[l].astype(f32)]
        used = n_in
        if l > 0:
            w_cols.append(w_in_vres[l - 1])
            mu_cols.append(mu_vres[l - 1].astype(f32))
            used += MV_LORA
        w_comb = jnp.pad(jnp.concatenate(w_cols, axis=1), ((0, 0), (0, N_COMB - used))).astype(bf16)
        mu = jnp.pad(jnp.concatenate(mu_cols), (0, N_COMB - used)).reshape(1, N_COMB)
        wup = _pad_rows(w_lora_up[l], 0, LO1_W).astype(bf16)
        aup = _pad_rows(a_lora_up[l], DECAY_LORA, LO1_W).astype(bf16)
        gup = _pad_rows(g_lora_up[l], 0, LO2_W).astype(bf16)
        vres = None
        if l > 0:
            vup = _pad_rows(v_lora_up[l - 1], GATE_LORA, LO2_W).astype(bf16)
            vres = (_row(v0[l - 1]), vup, v_first)
        outs = _mix_in(
            h, _row(ln1_g[l]), w_comb, mu, pool_w[l].astype(bf16), _row(pool_scale[l]),
            _row(w0[l]), wup, _row(a0[l]), aup, gup, _row(k_k[l]), _row(k_a[l]), _row(r_k[l]), vres)
        pool_o, scan_args = outs[0], outs[1:11]
        if l == 0:
            v_first = outs[11]
        rw = _scan(*scan_args, _row(gn_w[l]), _row(gn_b[l]))
        h = _out_ffn(h, pool_o, rw, w_out[l].astype(bf16), _row(ln2_g[l]),
                     w_gate_up[l].astype(bf16), w_down[l].astype(bf16), _row(final_g),
                     final=(l == depth - 1), seq=seq)
    return h
```

```python
import functools

import jax
import jax.numpy as jnp
from jax import lax
from jax.experimental import pallas as pl
from jax.experimental.pallas import tpu as pltpu

f32 = jnp.float32
bf16 = jnp.bfloat16

D_MODEL = 1024
N_META = 16
POOL_WIDTH = 512
POOL_WINDOWS = (2, 4, 8, 16)
POOL_GROUP = 128
RWKV_WIDTH = 512
HEAD_SIZE = 64
N_HEADS = 8
DECAY_LORA = 64
AAA_LORA = 64
MV_LORA = 32
GATE_LORA = 160
D_FF = 2816
RMS_EPS = 1e-6
GN_EPS = 64e-5
L2_EPS = 1e-12

C_R = POOL_WIDTH
C_K = C_R + RWKV_WIDTH
C_V = C_K + RWKV_WIDTH
C_LO1 = C_V + RWKV_WIDTH
C_LO2 = C_LO1 + DECAY_LORA + AAA_LORA
LO1_W = 128
LO2_W = 256
N_COMB = C_LO2 + LO2_W

CHUNK = 64
SEQ_PAD_TO = CHUNK * 33
HALO = 16
SUBLANES = 8
TT_MIX = 3 * CHUNK
TT_SCAN = 11 * CHUNK
TM_FFN = 512
FF_BLKS = (768, 768, 768, 512)
VMEM_LIMIT = 56 * 1024 * 1024

_HEAD_SLICES = [slice(h * HEAD_SIZE, (h + 1) * HEAD_SIZE) for h in range(N_HEADS)]


def _dot(a, b):
    return jnp.dot(a, b, preferred_element_type=f32)


def _dot_nt(a, b):
    return lax.dot_general(a, b, (((1,), (1,)), ((), ())), preferred_element_type=f32)


def _sigmoid(x):
    return 1.0 / (1.0 + jnp.exp(-x))


def _head_sums(x):
    return [jnp.sum(x[:, hs], axis=-1, keepdims=True) for hs in _HEAD_SLICES]


def _head_bcast(cols, low_half):
    return jnp.concatenate(
        [jnp.where(low_half, cols[2 * p], cols[2 * p + 1]) for p in range(N_HEADS // 2)], axis=1)


def _mix_in_kernel(*refs, has_vres, tt, n_tiles, tiles_per_row):
    if has_vres:
        (h_ref, g1_ref, w_ref, mu_ref, pw_ref, ps_ref, w0_ref, wup_ref, a0_ref, aup_ref,
         gup_ref, kk_ref, ka_ref, rk_ref, v0_ref, vup_ref, vf_ref,
         pool_ref, at_ref, rt_ref, bt_ref, kt_ref, bh_ref, kh_ref, vb_ref, bonus_ref, g_ref,
         egc_ref, e_ref, stash_ref) = refs
        vout_ref = None
    else:
        (h_ref, g1_ref, w_ref, mu_ref, pw_ref, ps_ref, w0_ref, wup_ref, a0_ref, aup_ref,
         gup_ref, kk_ref, ka_ref, rk_ref,
         pool_ref, at_ref, rt_ref, bt_ref, kt_ref, bh_ref, kh_ref, vb_ref, bonus_ref, g_ref,
         egc_ref, vout_ref, e_ref, stash_ref) = refs
    s = pl.program_id(0)
    j = jnp.minimum(s, n_tiles - 1) % tiles_per_row
    c = CHUNK

    @pl.when(s == 0)
    def _():
        stash_ref[...] = jnp.zeros(stash_ref.shape, f32)

    @pl.when(j == 0)
    def _():
        e_ref[0:HALO, :] = jnp.zeros((HALO, N_COMB), f32)

    prev = stash_ref.at[(s + 1) % 2]
    r, kraw, v, lw, a_sig = prev[0], prev[1], prev[2], prev[3], prev[4]
    row = lax.broadcasted_iota(jnp.int32, (c, c), 0)
    col = lax.broadcasted_iota(jnp.int32, (c, c), 1)
    tri_incl = (row >= col).astype(bf16)
    lw_hi = lw.astype(bf16)
    lw_r1 = lw - lw_hi.astype(f32)
    lw_mid = lw_r1.astype(bf16)
    lw_lo = (lw_r1 - lw_mid.astype(f32)).astype(bf16)
    g_parts, egc_parts = [], []
    for ci in range(tt // c):
        rows = slice(ci * c, (ci + 1) * c)
        g_ci = _dot(tri_incl, lw_hi[rows]) + _dot(tri_incl, lw_mid[rows]) + _dot(tri_incl, lw_lo[rows])
        egc = jnp.exp(g_ci[c - 1:c, :])
        egc_ref[0, ci * SUBLANES:(ci + 1) * SUBLANES, :] = jnp.broadcast_to(egc, (SUBLANES, RWKV_WIDTH))
        g_parts.append(g_ci)
        egc_parts.append(jnp.broadcast_to(egc, (c, RWKV_WIDTH)))
    g = jnp.concatenate(g_parts, axis=0)
    egc_rows = jnp.concatenate(egc_parts, axis=0)
    low_half = lax.broadcasted_iota(jnp.int32, (tt, 2 * HEAD_SIZE), 1) < HEAD_SIZE
    kkx = kraw * kk_ref[...]
    inv_norm = [lax.rsqrt(jnp.maximum(ss, L2_EPS * L2_EPS)) for ss in _head_sums(kkx * kkx)]
    kk = kkx * _head_bcast(inv_norm, low_half)
    kf = kraw * (1.0 + (a_sig - 1.0) * ka_ref[...])
    eng = jnp.exp(-g)
    kt = kf * eng
    bt = kk * a_sig * eng
    rt_ref[0] = (r * jnp.exp(g)).astype(bf16)
    at_ref[0] = (-(kk * jnp.exp(g - lw))).astype(bf16)
    kt_ref[0] = kt.astype(bf16)
    bt_ref[0] = bt.astype(bf16)
    kh_ref[0] = (kt * egc_rows).astype(bf16)
    bh_ref[0] = (bt * egc_rows).astype(bf16)
    vb_ref[0] = v.astype(bf16)
    bonus_ref[0] = _head_bcast(_head_sums(r * kf * rk_ref[...]), low_half) * v

    x = h_ref[0]
    hn = x * lax.rsqrt(jnp.mean(x * x, axis=-1, keepdims=True) + RMS_EPS) * g1_ref[...]
    e_ref[HALO:HALO + tt, :] = _dot(hn.astype(bf16), w_ref[...])

    pos = j * tt + lax.broadcasted_iota(jnp.int32, (tt, 1), 0)
    for gi, win in enumerate(POOL_WINDOWS):
        c0 = gi * POOL_GROUP
        u_ext = e_ref[:, c0:c0 + POOL_GROUP]
        acc = u_ext
        d = 1
        while d < win:
            acc = acc + pltpu.roll(acc, d, axis=0)
            d *= 2
        cnt = jnp.minimum(pos + 1, win).astype(f32)
        diff = acc[HALO:, :] / cnt - u_ext[HALO:, :]
        y = _dot(diff.astype(bf16), pw_ref[gi])
        pool_ref[0, :, c0:c0 + POOL_GROUP] = y * ps_ref[:, c0:c0 + POOL_GROUP]

    def shifted(c0, c1):
        cur = e_ref[HALO:HALO + tt, c0:c1]
        before = e_ref[HALO - 1:HALO - 1 + tt, c0:c1]
        return cur + (before - cur) * mu_ref[:, c0:c1]

    nxt = stash_ref.at[s % 2]
    nxt[0] = shifted(C_R, C_K)
    nxt[1] = shifted(C_K, C_V)
    v_new = shifted(C_V, C_LO1)
    lo1 = shifted(C_LO1, C_LO2)
    lo2 = shifted(C_LO2, N_COMB)
    zw = w0_ref[...] + _dot(jnp.tanh(lo1).astype(bf16), wup_ref[...])
    nxt[3] = (-jnp.exp(f32(-0.5))) * _sigmoid(zw)
    nxt[4] = _sigmoid(a0_ref[...] + _dot(lo1.astype(bf16), aup_ref[...]))
    g_ref[0] = _dot(_sigmoid(lo2).astype(bf16), gup_ref[...])
    if has_vres:
        v_mix = _sigmoid(v0_ref[...] + _dot(lo2.astype(bf16), vup_ref[...]))
        v_new = v_new + (vf_ref[0] - v_new) * v_mix
    else:
        vout_ref[0] = v_new
    nxt[2] = v_new

    e_ref[0:HALO, :] = jnp.where(s < n_tiles - 1, e_ref[tt:tt + HALO, :], e_ref[0:HALO, :])


def _mix_in(h, g1, w_comb, mu, pool_w, pool_scale, w0, wup, a0, aup, gup, k_k, k_a, r_k, vres):
    bsz, tp, _ = h.shape
    tt = TT_MIX
    tiles_per_row = tp // tt
    n_tiles = bsz * tiles_per_row
    egc_rows = tt // CHUNK * SUBLANES
    has_vres = vres is not None

    def cur_map(s):
        t = jnp.minimum(s, n_tiles - 1)
        return (t // tiles_per_row, t % tiles_per_row, 0)

    def lag_map(s):
        t = jnp.maximum(s - 1, 0)
        return (t // tiles_per_row, t % tiles_per_row, 0)

    cur = lambda width: pl.BlockSpec((1, tt, width), cur_map)
    lag = lambda width: pl.BlockSpec((1, tt, width), lag_map)
    cspec = lambda shape: pl.BlockSpec(shape, lambda s: (0,) * len(shape), pipeline_mode=pl.Buffered(1))
    in_specs = [
        cur(D_MODEL), cspec((1, D_MODEL)), cspec((D_MODEL, N_COMB)), cspec((1, N_COMB)),
        cspec((len(POOL_WINDOWS), POOL_GROUP, POOL_GROUP)),
        cspec((1, POOL_WIDTH)), cspec((1, RWKV_WIDTH)), cspec((LO1_W, RWKV_WIDTH)),
        cspec((1, RWKV_WIDTH)), cspec((LO1_W, RWKV_WIDTH)), cspec((LO2_W, RWKV_WIDTH)),
        cspec((1, RWKV_WIDTH)), cspec((1, RWKV_WIDTH)), cspec((1, RWKV_WIDTH)),
    ]
    args = [h, g1, w_comb, mu, pool_w, pool_scale, w0, wup, a0, aup, gup, k_k, k_a, r_k]
    wide = lambda dtype: jax.ShapeDtypeStruct((bsz, tp, RWKV_WIDTH), dtype)
    out_shape = [wide(f32)] + [wide(bf16)] * 7 + [wide(f32), wide(f32),
                 jax.ShapeDtypeStruct((bsz, tp // CHUNK * SUBLANES, RWKV_WIDTH), f32)]
    out_specs = [cur(RWKV_WIDTH)] + [lag(RWKV_WIDTH)] * 8 + [
        cur(RWKV_WIDTH), pl.BlockSpec((1, egc_rows, RWKV_WIDTH), lag_map)]
    if has_vres:
        v0, vup, v_first = vres
        in_specs += [cspec((1, RWKV_WIDTH)), cspec((LO2_W, RWKV_WIDTH)), cur(RWKV_WIDTH)]
        args += [v0, vup, v_first]
    else:
        out_shape.append(wide(f32))
        out_specs.append(cur(RWKV_WIDTH))
    return pl.pallas_call(
        functools.partial(_mix_in_kernel, has_vres=has_vres, tt=tt, n_tiles=n_tiles,
                          tiles_per_row=tiles_per_row),
        out_shape=tuple(out_shape),
        grid=(n_tiles + 1,),
        in_specs=in_specs,
        out_specs=tuple(out_specs),
        scratch_shapes=[pltpu.VMEM((HALO + tt, N_COMB), f32),
                        pltpu.VMEM((2, 5, tt, RWKV_WIDTH), f32)],
        compiler_params=pltpu.CompilerParams(
            dimension_semantics=("arbitrary",), vmem_limit_bytes=VMEM_LIMIT),
        name="mix_in_vres" if has_vres else "mix_in",
    )(*args)


def _scan_kernel(at_ref, rt_ref, bt_ref, kt_ref, bh_ref, kh_ref, egc_ref,
                 vb_ref, bonus_ref, g_ref, gnw_ref, gnb_ref,
                 o_ref, z_ref, wb_ref, zy_ref, dcol_ref, *, ts, tiles_per_row):
    s = pl.program_id(0)
    wr = s % 2
    rd = (s + 1) % 2

    @pl.when(s == 0)
    def _():
        wb_ref[...] = jnp.zeros(wb_ref.shape, bf16)
        zy_ref[...] = jnp.zeros(zy_ref.shape, bf16)
        dcol_ref[...] = jnp.zeros(dcol_ref.shape, f32)

    @pl.when((s + tiles_per_row - 1) % tiles_per_row == 0)
    def _():
        z_ref[...] = jnp.zeros(z_ref.shape, f32)

    c = CHUNK
    n_chunks = ts // c
    row2 = lax.broadcasted_iota(jnp.int32, (2 * c, 2 * c), 0)
    col2 = lax.broadcasted_iota(jnp.int32, (2 * c, 2 * c), 1)
    rm, cm = row2 & (c - 1), col2 & (c - 1)
    a_mask = (rm > cm) | ((row2 >= c) & (rm == cm))
    rowv = lax.broadcasted_iota(jnp.int32, (c, 2 * c), 0)
    colv = lax.broadcasted_iota(jnp.int32, (c, 2 * c), 1)
    low_half = colv < c
    heads = range(N_HEADS)
    hsl = _HEAD_SLICES

    def chunk_rows(ci):
        return pl.ds(pl.multiple_of(ci * c, c), c)

    def solve_products(cis):
        ws, xs = {}, {}
        pair_lo = lax.broadcasted_iota(jnp.int32, (2 * c, 2 * c), 1) < c
        zeros_cc = jnp.zeros((c, c), bf16)
        eye_hi = (colv == rowv + c).astype(f32)
        for n, ci in enumerate(cis):
            rows = chunk_rows(ci)
            at, rt, bt, kt = at_ref[0, rows, :], rt_ref[0, rows, :], bt_ref[0, rows, :], kt_ref[0, rows, :]
            bh, kh = bh_ref[0, rows, :], kh_ref[0, rows, :]
            egc = egc_ref[0, pl.ds(pl.multiple_of(ci * SUBLANES, SUBLANES), SUBLANES), :][0:1, :]
            for p in range(N_HEADS // 2):
                lanes = slice(2 * c * p, 2 * c * (p + 1))
                la = jnp.concatenate([at[:, lanes], rt[:, lanes]], axis=0)
                ra = jnp.concatenate([bt[:, lanes], kt[:, lanes]], axis=0)
                zero = jnp.zeros_like(la)
                am2 = _dot_nt(jnp.concatenate([jnp.where(pair_lo, la, zero),
                                               jnp.where(pair_lo, zero, la)], axis=0), ra)
                hat2 = jnp.transpose(jnp.concatenate([bh[:, lanes], kh[:, lanes]], axis=0))
                dcol = jnp.transpose(jnp.broadcast_to(egc[:, lanes], (c, 2 * c)))
                at_pair = at[:, lanes].astype(f32)
                at_low = [at_pair, pltpu.roll(at_pair, c, axis=1)]
                for q in range(2):
                    h = 2 * p + q
                    amat = jnp.where(a_mask, am2[2 * c * q:2 * c * (q + 1), :], 0.0)
                    top = amat[0:c, :]
                    ws[n, h] = jnp.where(low_half, top, eye_hi)
                    xs[n, h] = jnp.where(low_half, at_low[q], top).astype(bf16)
                    zy_ref[wr, ci, h] = jnp.concatenate(
                        [jnp.concatenate([hat2[c * q:c * (q + 1), :], zeros_cc], axis=1),
                         jnp.concatenate([amat[c:2 * c, :].astype(bf16), rt[:, hsl[h]]], axis=1)], axis=0)
                    dcol_ref[wr, ci, h] = dcol[c * q:c * (q + 1), :]
        return ws, xs

    def solve_level(ws, lvl):
        same = low_half & ((rowv >> (lvl + 1)) == (colv >> (lvl + 1)))
        for key in ws:
            w = ws[key]
            dmat = jnp.where(same, w, 0.0)[:, 0:c]
            ws[key] = jnp.where(same, 0.0, w) + _dot(dmat.astype(bf16), w.astype(bf16))

    def solve_store(ws, xs, cis):
        zeros_rows = jnp.zeros((c, 2 * c), bf16)
        for (n, h), w in ws.items():
            wb_ref[wr, cis[n], h] = _dot(
                w.astype(bf16), jnp.concatenate([zeros_rows, xs[n, h]], axis=0)).astype(bf16)

    def recur_u(ci):
        vb = vb_ref[0, chunk_rows(ci), :]
        zs = [z_ref[h] for h in heads]
        zbs = [z.astype(bf16) for z in zs]
        vbs = [vb[:, hsl[h]] for h in heads]
        ubs = [_dot(wb_ref[rd, ci, h], jnp.concatenate([zbs[h], vbs[h]], axis=0)).astype(bf16)
               for h in heads]
        return zs, zbs, vbs, ubs

    def recur_zy(ci, ctx):
        zs, zbs, vbs, ubs = ctx
        ys = []
        for h in heads:
            zy = _dot(zy_ref[rd, ci, h], jnp.concatenate([ubs[h], vbs[h], zbs[h]], axis=0))
            z_ref[h] = dcol_ref[rd, ci, h] * zs[h] + zy[0:c, :]
            ys.append(zy[c:2 * c, :])
        rows = chunk_rows(ci)
        ycs = [y - jnp.mean(y, axis=-1, keepdims=True) for y in ys]
        rstd = [lax.rsqrt(jnp.mean(yc * yc, axis=-1, keepdims=True) + GN_EPS) for yc in ycs]
        yn = jnp.concatenate(ycs, axis=1) * _head_bcast(rstd, low_half) * gnw_ref[...] + gnb_ref[...]
        o_ref[0, rows, :] = (yn + bonus_ref[0, rows, :]) * g_ref[0, rows, :]

    def group(cis):
        state = {}

        def products(n):
            state[n] = solve_products([cis[n]])

        def level(lvl, n):
            solve_level(state[n][0], lvl)

        def store(n):
            solve_store(state[n][0], state[n][1], [cis[n]])

        members = range(len(cis))
        solve = ([functools.partial(products, n) for n in members]
                 + [functools.partial(level, lvl, n) for lvl in range(6) for n in members]
                 + [functools.partial(store, n) for n in members])
        ctx = [None]

        def recur_first(ci):
            ctx[0] = recur_u(ci)

        def recur_second(ci):
            recur_zy(ci, ctx[0])

        recur = []
        for ci in cis:
            recur += [functools.partial(recur_first, ci), functools.partial(recur_second, ci)]
        assert len(recur) <= len(solve)
        for k, stage in enumerate(solve):
            if k < len(recur):
                recur[k]()
            stage()

    n_pairs = (-n_chunks) % 3
    n_triples = (n_chunks - 2 * n_pairs) // 3
    assert n_triples >= 0

    def triple_body(i, carry):
        group([3 * i, 3 * i + 1, 3 * i + 2])
        return carry

    lax.fori_loop(0, n_triples, triple_body, 0)
    for k in range(n_pairs):
        group([3 * n_triples + 2 * k, 3 * n_triples + 2 * k + 1])


def _scan(at, rt, bt, kt, bh, kh, vb, bonus, g, egc, gn_w, gn_b):
    bsz, tp, _ = at.shape
    ts = TT_SCAN
    n_chunks = ts // CHUNK
    tiles_per_row = tp // ts
    n_tiles = bsz * tiles_per_row

    def cur_map(s):
        t = jnp.minimum(s, n_tiles - 1)
        return (t // tiles_per_row, t % tiles_per_row, 0)

    def lag_map(s):
        t = jnp.maximum(s - 1, 0)
        return (t // tiles_per_row, t % tiles_per_row, 0)

    cur = pl.BlockSpec((1, ts, RWKV_WIDTH), cur_map)
    lag = pl.BlockSpec((1, ts, RWKV_WIDTH), lag_map)
    ecur = pl.BlockSpec((1, n_chunks * SUBLANES, RWKV_WIDTH), cur_map)
    cspec = pl.BlockSpec((1, RWKV_WIDTH), lambda s: (0, 0))
    return pl.pallas_call(
        functools.partial(_scan_kernel, ts=ts, tiles_per_row=tiles_per_row),
        out_shape=jax.ShapeDtypeStruct((bsz, tp, RWKV_WIDTH), f32),
        grid=(n_tiles + 1,),
        in_specs=[cur] * 6 + [ecur] + [lag] * 3 + [cspec] * 2,
        out_specs=lag,
        scratch_shapes=[
            pltpu.VMEM((N_HEADS, HEAD_SIZE, HEAD_SIZE), f32),
            pltpu.VMEM((2, n_chunks, N_HEADS, CHUNK, 2 * CHUNK), bf16),
            pltpu.VMEM((2, n_chunks, N_HEADS, 2 * CHUNK, 3 * CHUNK), bf16),
            pltpu.VMEM((2, n_chunks, N_HEADS, HEAD_SIZE, HEAD_SIZE), f32),
        ],
        compiler_params=pltpu.CompilerParams(
            dimension_semantics=("arbitrary",), vmem_limit_bytes=VMEM_LIMIT),
        name="rwkv_scan",
    )(at, rt, bt, kt, bh, kh, egc, vb, bonus, g, gn_w, gn_b)


def _out_ffn_kernel(h_ref, pool_ref, rw_ref, wo_ref, g2_ref, wg_ref, wu_ref, wd_ref, gf_ref,
                    o_ref, *, final):
    load = (lambda ref: ref[0]) if final else (lambda ref: ref[...])
    mixed = (_dot(load(pool_ref).astype(bf16), wo_ref[0:POOL_WIDTH, :])
             + _dot(load(rw_ref).astype(bf16), wo_ref[POOL_WIDTH:, :]))
    h1 = load(h_ref) + mixed
    hn = h1 * lax.rsqrt(jnp.mean(h1 * h1, axis=-1, keepdims=True) + RMS_EPS) * g2_ref[...]
    hb = hn.astype(bf16)
    acc = h1
    start = 0
    for width in FF_BLKS:
        cs = slice(start, start + width)
        start += width
        gate = _dot(hb, wg_ref[:, cs])
        up = _dot(hb, wu_ref[:, cs])
        act = gate * _sigmoid(gate) * up
        acc = acc + _dot(act.astype(bf16), wd_ref[cs, :])
    if final:
        o_ref[0] = acc * lax.rsqrt(jnp.mean(acc * acc, axis=-1, keepdims=True) + RMS_EPS) * gf_ref[...]
    else:
        o_ref[...] = acc


def _out_ffn(h, pool, rw, w_out, g2, w_gate_up, w_down, g_final, layer, final, seq):
    bsz, tp, _ = h.shape
    tm = TM_FFN
    assert sum(FF_BLKS) == D_FF
    cspec = lambda shape: pl.BlockSpec(shape, lambda *_: (0, 0), pipeline_mode=pl.Buffered(1))
    stacked = lambda shape, half=0: pl.BlockSpec(
        (None,) + shape, lambda *_: (layer, 0, half), pipeline_mode=pl.Buffered(1))
    weights = [stacked((D_MODEL, D_MODEL)), cspec((1, D_MODEL)),
               stacked((D_MODEL, D_FF), 0), stacked((D_MODEL, D_FF), 1),
               stacked((D_FF, D_MODEL)), cspec((1, D_MODEL))]
    if final:
        assert seq % tm == 0 and N_META % SUBLANES == 0
        win = lambda width: pl.BlockSpec(
            (pl.Element(1), pl.Element(tm), pl.Element(width)),
            lambda b, i: (b, pl.multiple_of(N_META + i * tm, SUBLANES), 0))
        args = (h, pool, rw)
        grid = (bsz, seq // tm)
        out_shape = jax.ShapeDtypeStruct((bsz, seq, D_MODEL), f32)
        out_spec = pl.BlockSpec((1, tm, D_MODEL), lambda b, i: (b, i, 0))
        sem = ("arbitrary", "arbitrary")
    else:
        rows = bsz * tp
        win = lambda width: pl.BlockSpec((tm, width), lambda i: (i, 0))
        args = tuple(t.reshape(rows, t.shape[-1]) for t in (h, pool, rw))
        grid = (rows // tm,)
        out_shape = jax.ShapeDtypeStruct((rows, D_MODEL), f32)
        out_spec = win(D_MODEL)
        sem = ("arbitrary",)
    out = pl.pallas_call(
        functools.partial(_out_ffn_kernel, final=final),
        out_shape=out_shape,
        grid=grid,
        in_specs=[win(D_MODEL), win(POOL_WIDTH), win(RWKV_WIDTH)] + weights,
        out_specs=out_spec,
        compiler_params=pltpu.CompilerParams(dimension_semantics=sem, vmem_limit_bytes=VMEM_LIMIT),
        name="out_ffn_final" if final else "out_ffn",
    )(*args, w_out, g2, w_gate_up, w_gate_up, w_down, g_final)
    return out if final else out.reshape(bsz, tp, D_MODEL)


def _row(p):
    return p.reshape(1, -1).astype(f32)


def _pad_rows(w, before, total):
    return jnp.pad(w, ((before, total - before - w.shape[0]), (0, 0)))


def kernel(x, meta_tokens, ln1_g, w_in, mu_shift, pool_w, pool_scale, w0, w_lora_up, a0, a_lora_up, g_lora_up, k_k, k_a, r_k, gn_w, gn_b, w_in_vres, mu_vres, v0, v_lora_up, w_out, ln2_g, w_gate_up, w_down, final_g):
    bsz, seq, _ = x.shape
    depth = w_in.shape[0]
    t_real = N_META + seq
    tp = SEQ_PAD_TO
    assert t_real <= tp and tp % TT_MIX == 0 and tp % TT_SCAN == 0 and (bsz * tp) % TM_FFN == 0
    meta = jnp.broadcast_to(meta_tokens.astype(x.dtype)[None], (bsz, N_META, D_MODEL))
    h = jnp.concatenate([meta, x, jnp.zeros((bsz, tp - t_real, D_MODEL), x.dtype)], axis=1)

    w_out_b, w_gate_up_b, w_down_b = (w.astype(bf16) for w in (w_out, w_gate_up, w_down))
    v_first = None
    for l in range(depth):
        n_in = w_in.shape[2]
        w_cols = [w_in[l]]
        mu_cols = [jnp.zeros((POOL_WIDTH,), f32), mu_shift[l].astype(f32)]
        used = n_in
        if l > 0:
            w_cols.append(w_in_vres[l - 1])
            mu_cols.append(mu_vres[l - 1].astype(f32))
            used += MV_LORA
        w_comb = jnp.pad(jnp.concatenate(w_cols, axis=1), ((0, 0), (0, N_COMB - used))).astype(bf16)
        mu = jnp.pad(jnp.concatenate(mu_cols), (0, N_COMB - used)).reshape(1, N_COMB)
        wup = _pad_rows(w_lora_up[l], 0, LO1_W).astype(bf16)
        aup = _pad_rows(a_lora_up[l], DECAY_LORA, LO1_W).astype(bf16)
        gup = _pad_rows(g_lora_up[l], 0, LO2_W).astype(bf16)
        vres = None
        if l > 0:
            vup = _pad_rows(v_lora_up[l - 1], GATE_LORA, LO2_W).astype(bf16)
            vres = (_row(v0[l - 1]), vup, v_first)
        outs = _mix_in(
            h, _row(ln1_g[l]), w_comb, mu, pool_w[l].astype(bf16), _row(pool_scale[l]),
            _row(w0[l]), wup, _row(a0[l]), aup, gup, _row(k_k[l]), _row(k_a[l]), _row(r_k[l]), vres)
        pool_o, scan_args = outs[0], outs[1:11]
        if l == 0:
            v_first = outs[11]
        rw = _scan(*scan_args, _row(gn_w[l]), _row(gn_b[l]))
        h = _out_ffn(h, pool_o, rw, w_out_b, _row(ln2_g[l]), w_gate_up_b, w_down_b, _row(final_g),
                     layer=l, final=(l == depth - 1), seq=seq)
    return h
```

```python
import functools

import jax
import jax.numpy as jnp
from jax import lax
from jax.experimental import pallas as pl
from jax.experimental.pallas import tpu as pltpu

f32 = jnp.float32
bf16 = jnp.bfloat16

D_MODEL = 1024
N_META = 16
POOL_WIDTH = 512
POOL_WINDOWS = (2, 4, 8, 16)
POOL_GROUP = 128
RWKV_WIDTH = 512
HEAD_SIZE = 64
N_HEADS = 8
DECAY_LORA = 64
AAA_LORA = 64
MV_LORA = 32
GATE_LORA = 160
D_FF = 2816
RMS_EPS = 1e-6
GN_EPS = 64e-5
L2_EPS = 1e-12

C_R = POOL_WIDTH
C_K = C_R + RWKV_WIDTH
C_V = C_K + RWKV_WIDTH
C_LO1 = C_V + RWKV_WIDTH
C_LO2 = C_LO1 + DECAY_LORA + AAA_LORA
LO1_W = 128
LO2_W = 256
N_COMB = C_LO2 + LO2_W

CHUNK = 64
SEQ_PAD_TO = CHUNK * 33
HALO = 16
SUBLANES = 8
TT_MIX = 3 * CHUNK
TT_SCAN = 11 * CHUNK
SCAN_GROUP = 4
TM_FFN = 512
FF_BLKS = (768, 768, 768, 512)
VMEM_LIMIT = 56 * 1024 * 1024

_HEAD_SLICES = [slice(h * HEAD_SIZE, (h + 1) * HEAD_SIZE) for h in range(N_HEADS)]


def _dot(a, b):
    return jnp.dot(a, b, preferred_element_type=f32)


def _dot_nt(a, b):
    return lax.dot_general(a, b, (((1,), (1,)), ((), ())), preferred_element_type=f32)


def _sigmoid(x):
    return 1.0 / (1.0 + jnp.exp(-x))


def _head_sums(x):
    return [jnp.sum(x[:, hs], axis=-1, keepdims=True) for hs in _HEAD_SLICES]


def _head_bcast(cols, low_half):
    return jnp.concatenate(
        [jnp.where(low_half, cols[2 * p], cols[2 * p + 1]) for p in range(N_HEADS // 2)], axis=1)


def _mix_in_kernel(*refs, has_vres, tt, n_tiles, tiles_per_row):
    if has_vres:
        (h_ref, g1_ref, w_ref, mu_ref, pw_ref, ps_ref, w0_ref, wup_ref, a0_ref, aup_ref,
         gup_ref, kk_ref, ka_ref, rk_ref, v0_ref, vup_ref, vf_ref,
         pool_ref, at_ref, rt_ref, bt_ref, kt_ref, bh_ref, kh_ref, vb_ref, bonus_ref, g_ref,
         egc_ref, e_ref, stash_ref) = refs
        vout_ref = None
    else:
        (h_ref, g1_ref, w_ref, mu_ref, pw_ref, ps_ref, w0_ref, wup_ref, a0_ref, aup_ref,
         gup_ref, kk_ref, ka_ref, rk_ref,
         pool_ref, at_ref, rt_ref, bt_ref, kt_ref, bh_ref, kh_ref, vb_ref, bonus_ref, g_ref,
         egc_ref, vout_ref, e_ref, stash_ref) = refs
    s = pl.program_id(0)
    j = jnp.minimum(s, n_tiles - 1) % tiles_per_row
    c = CHUNK

    @pl.when(s == 0)
    def _():
        stash_ref[...] = jnp.zeros(stash_ref.shape, f32)

    @pl.when(j == 0)
    def _():
        e_ref[0:HALO, :] = jnp.zeros((HALO, N_COMB), f32)

    prev = stash_ref.at[(s + 1) % 2]
    r, kraw, v, lw, a_sig = prev[0], prev[1], prev[2], prev[3], prev[4]
    row = lax.broadcasted_iota(jnp.int32, (c, c), 0)
    col = lax.broadcasted_iota(jnp.int32, (c, c), 1)
    tri_incl = (row >= col).astype(bf16)
    lw_hi = lw.astype(bf16)
    lw_r1 = lw - lw_hi.astype(f32)
    lw_mid = lw_r1.astype(bf16)
    lw_lo = (lw_r1 - lw_mid.astype(f32)).astype(bf16)
    g_parts, egc_parts = [], []
    for ci in range(tt // c):
        rows = slice(ci * c, (ci + 1) * c)
        g_ci = _dot(tri_incl, lw_hi[rows]) + _dot(tri_incl, lw_mid[rows]) + _dot(tri_incl, lw_lo[rows])
        egc = jnp.exp(g_ci[c - 1:c, :])
        egc_ref[0, ci * SUBLANES:(ci + 1) * SUBLANES, :] = jnp.broadcast_to(egc, (SUBLANES, RWKV_WIDTH))
        g_parts.append(g_ci)
        egc_parts.append(jnp.broadcast_to(egc, (c, RWKV_WIDTH)))
    g = jnp.concatenate(g_parts, axis=0)
    egc_rows = jnp.concatenate(egc_parts, axis=0)
    low_half = lax.broadcasted_iota(jnp.int32, (tt, 2 * HEAD_SIZE), 1) < HEAD_SIZE
    kkx = kraw * kk_ref[...]
    inv_norm = [lax.rsqrt(jnp.maximum(ss, L2_EPS * L2_EPS)) for ss in _head_sums(kkx * kkx)]
    kk = kkx * _head_bcast(inv_norm, low_half)
    kf = kraw * (1.0 + (a_sig - 1.0) * ka_ref[...])
    eng = jnp.exp(-g)
    kt = kf * eng
    bt = kk * a_sig * eng
    rt_ref[0] = (r * jnp.exp(g)).astype(bf16)
    at_ref[0] = (-(kk * jnp.exp(g - lw))).astype(bf16)
    kt_ref[0] = kt.astype(bf16)
    bt_ref[0] = bt.astype(bf16)
    kh_ref[0] = (kt * egc_rows).astype(bf16)
    bh_ref[0] = (bt * egc_rows).astype(bf16)
    vb_ref[0] = v.astype(bf16)
    bonus_ref[0] = _head_bcast(_head_sums(r * kf * rk_ref[...]), low_half) * v

    x = h_ref[0]
    hn = x * lax.rsqrt(jnp.mean(x * x, axis=-1, keepdims=True) + RMS_EPS) * g1_ref[...]
    e_ref[HALO:HALO + tt, :] = _dot(hn.astype(bf16), w_ref[...])

    pos = j * tt + lax.broadcasted_iota(jnp.int32, (tt, 1), 0)
    for gi, win in enumerate(POOL_WINDOWS):
        c0 = gi * POOL_GROUP
        u_ext = e_ref[:, c0:c0 + POOL_GROUP]
        acc = u_ext
        d = 1
        while d < win:
            acc = acc + pltpu.roll(acc, d, axis=0)
            d *= 2
        cnt = jnp.minimum(pos + 1, win).astype(f32)
        diff = acc[HALO:, :] / cnt - u_ext[HALO:, :]
        y = _dot(diff.astype(bf16), pw_ref[gi])
        pool_ref[0, :, c0:c0 + POOL_GROUP] = y * ps_ref[:, c0:c0 + POOL_GROUP]

    def shifted(c0, c1):
        cur = e_ref[HALO:HALO + tt, c0:c1]
        before = e_ref[HALO - 1:HALO - 1 + tt, c0:c1]
        return cur + (before - cur) * mu_ref[:, c0:c1]

    nxt = stash_ref.at[s % 2]
    nxt[0] = shifted(C_R, C_K)
    nxt[1] = shifted(C_K, C_V)
    v_new = shifted(C_V, C_LO1)
    lo1 = shifted(C_LO1, C_LO2)
    lo2 = shifted(C_LO2, N_COMB)
    zw = w0_ref[...] + _dot(jnp.tanh(lo1).astype(bf16), wup_ref[...])
    nxt[3] = (-jnp.exp(f32(-0.5))) * _sigmoid(zw)
    nxt[4] = _sigmoid(a0_ref[...] + _dot(lo1.astype(bf16), aup_ref[...]))
    g_ref[0] = _dot(_sigmoid(lo2).astype(bf16), gup_ref[...])
    if has_vres:
        v_mix = _sigmoid(v0_ref[...] + _dot(lo2.astype(bf16), vup_ref[...]))
        v_new = v_new + (vf_ref[0] - v_new) * v_mix
    else:
        vout_ref[0] = v_new
    nxt[2] = v_new

    e_ref[0:HALO, :] = jnp.where(s < n_tiles - 1, e_ref[tt:tt + HALO, :], e_ref[0:HALO, :])


def _mix_in(h, g1, w_comb, mu, pool_w, pool_scale, w0, wup, a0, aup, gup, k_k, k_a, r_k, vres):
    bsz, tp, _ = h.shape
    tt = TT_MIX
    tiles_per_row = tp // tt
    n_tiles = bsz * tiles_per_row
    egc_rows = tt // CHUNK * SUBLANES
    has_vres = vres is not None

    def cur_map(s):
        t = jnp.minimum(s, n_tiles - 1)
        return (t // tiles_per_row, t % tiles_per_row, 0)

    def lag_map(s):
        t = jnp.maximum(s - 1, 0)
        return (t // tiles_per_row, t % tiles_per_row, 0)

    cur = lambda width: pl.BlockSpec((1, tt, width), cur_map)
    lag = lambda width: pl.BlockSpec((1, tt, width), lag_map)
    cspec = lambda shape: pl.BlockSpec(shape, lambda s: (0,) * len(shape), pipeline_mode=pl.Buffered(1))
    in_specs = [
        cur(D_MODEL), cspec((1, D_MODEL)), cspec((D_MODEL, N_COMB)), cspec((1, N_COMB)),
        cspec((len(POOL_WINDOWS), POOL_GROUP, POOL_GROUP)),
        cspec((1, POOL_WIDTH)), cspec((1, RWKV_WIDTH)), cspec((LO1_W, RWKV_WIDTH)),
        cspec((1, RWKV_WIDTH)), cspec((LO1_W, RWKV_WIDTH)), cspec((LO2_W, RWKV_WIDTH)),
        cspec((1, RWKV_WIDTH)), cspec((1, RWKV_WIDTH)), cspec((1, RWKV_WIDTH)),
    ]
    args = [h, g1, w_comb, mu, pool_w, pool_scale, w0, wup, a0, aup, gup, k_k, k_a, r_k]
    wide = lambda dtype: jax.ShapeDtypeStruct((bsz, tp, RWKV_WIDTH), dtype)
    out_shape = [wide(f32)] + [wide(bf16)] * 7 + [wide(f32), wide(f32),
                 jax.ShapeDtypeStruct((bsz, tp // CHUNK * SUBLANES, RWKV_WIDTH), f32)]
    out_specs = [cur(RWKV_WIDTH)] + [lag(RWKV_WIDTH)] * 8 + [
        cur(RWKV_WIDTH), pl.BlockSpec((1, egc_rows, RWKV_WIDTH), lag_map)]
    if has_vres:
        v0, vup, v_first = vres
        in_specs += [cspec((1, RWKV_WIDTH)), cspec((LO2_W, RWKV_WIDTH)), cur(RWKV_WIDTH)]
        args += [v0, vup, v_first]
    else:
        out_shape.append(wide(f32))
        out_specs.append(cur(RWKV_WIDTH))
    return pl.pallas_call(
        functools.partial(_mix_in_kernel, has_vres=has_vres, tt=tt, n_tiles=n_tiles,
                          tiles_per_row=tiles_per_row),
        out_shape=tuple(out_shape),
        grid=(n_tiles + 1,),
        in_specs=in_specs,
        out_specs=tuple(out_specs),
        scratch_shapes=[pltpu.VMEM((HALO + tt, N_COMB), f32),
                        pltpu.VMEM((2, 5, tt, RWKV_WIDTH), f32)],
        compiler_params=pltpu.CompilerParams(
            dimension_semantics=("arbitrary",), vmem_limit_bytes=VMEM_LIMIT),
        name="mix_in_vres" if has_vres else "mix_in",
    )(*args)


def _scan_kernel(at_ref, rt_ref, bt_ref, kt_ref, bh_ref, kh_ref, egc_ref,
                 vb_ref, bonus_ref, g_ref, gnw_ref, gnb_ref,
                 o_ref, z_ref, wb_ref, zy_ref, dcol_ref, *, ts, tiles_per_row):
    s = pl.program_id(0)
    wr = s % 2
    rd = (s + 1) % 2

    @pl.when(s == 0)
    def _():
        wb_ref[...] = jnp.zeros(wb_ref.shape, bf16)
        zy_ref[...] = jnp.zeros(zy_ref.shape, bf16)
        dcol_ref[...] = jnp.zeros(dcol_ref.shape, f32)

    @pl.when((s + tiles_per_row - 1) % tiles_per_row == 0)
    def _():
        z_ref[...] = jnp.zeros(z_ref.shape, f32)

    c = CHUNK
    n_chunks = ts // c
    row2 = lax.broadcasted_iota(jnp.int32, (2 * c, 2 * c), 0)
    col2 = lax.broadcasted_iota(jnp.int32, (2 * c, 2 * c), 1)
    rm, cm = row2 & (c - 1), col2 & (c - 1)
    a_mask = (rm > cm) | ((row2 >= c) & (rm == cm))
    rowv = lax.broadcasted_iota(jnp.int32, (c, 2 * c), 0)
    colv = lax.broadcasted_iota(jnp.int32, (c, 2 * c), 1)
    low_half = colv < c
    heads = range(N_HEADS)
    hsl = _HEAD_SLICES

    def chunk_rows(ci):
        return pl.ds(pl.multiple_of(ci * c, c), c)

    def solve_products(cis):
        ws, xs = {}, {}
        pair_lo = lax.broadcasted_iota(jnp.int32, (2 * c, 2 * c), 1) < c
        zeros_cc = jnp.zeros((c, c), bf16)
        eye_hi = (colv == rowv + c).astype(f32)
        for n, ci in enumerate(cis):
            rows = chunk_rows(ci)
            at, rt, bt, kt = at_ref[0, rows, :], rt_ref[0, rows, :], bt_ref[0, rows, :], kt_ref[0, rows, :]
            bh, kh = bh_ref[0, rows, :], kh_ref[0, rows, :]
            egc = egc_ref[0, pl.ds(pl.multiple_of(ci * SUBLANES, SUBLANES), SUBLANES), :][0:1, :]
            for p in range(N_HEADS // 2):
                lanes = slice(2 * c * p, 2 * c * (p + 1))
                la = jnp.concatenate([at[:, lanes], rt[:, lanes]], axis=0)
                ra = jnp.concatenate([bt[:, lanes], kt[:, lanes]], axis=0)
                zero = jnp.zeros_like(la)
                am2 = _dot_nt(jnp.concatenate([jnp.where(pair_lo, la, zero),
                                               jnp.where(pair_lo, zero, la)], axis=0), ra)
                hat2 = jnp.transpose(jnp.concatenate([bh[:, lanes], kh[:, lanes]], axis=0))
                dcol = jnp.transpose(jnp.broadcast_to(egc[:, lanes], (c, 2 * c)))
                at_pair = at[:, lanes].astype(f32)
                at_low = [at_pair, pltpu.roll(at_pair, c, axis=1)]
                for q in range(2):
                    h = 2 * p + q
                    amat = jnp.where(a_mask, am2[2 * c * q:2 * c * (q + 1), :], 0.0)
                    top = amat[0:c, :]
                    ws[n, h] = jnp.where(low_half, top, eye_hi)
                    xs[n, h] = jnp.where(low_half, at_low[q], top).astype(bf16)
                    zy_ref[wr, ci, h] = jnp.concatenate(
                        [jnp.concatenate([hat2[c * q:c * (q + 1), :], zeros_cc], axis=1),
                         jnp.concatenate([amat[c:2 * c, :].astype(bf16), rt[:, hsl[h]]], axis=1)], axis=0)
                    dcol_ref[wr, ci, h] = dcol[c * q:c * (q + 1), :]
        return ws, xs

    def solve_level(ws, lvl):
        same = low_half & ((rowv >> (lvl + 1)) == (colv >> (lvl + 1)))
        for key in ws:
            w = ws[key]
            dmat = jnp.where(same, w, 0.0)[:, 0:c]
            ws[key] = jnp.where(same, 0.0, w) + _dot(dmat.astype(bf16), w.astype(bf16))

    def solve_store(ws, xs, cis):
        zeros_rows = jnp.zeros((c, 2 * c), bf16)
        for (n, h), w in ws.items():
            wb_ref[wr, cis[n], h] = _dot(
                w.astype(bf16), jnp.concatenate([zeros_rows, xs[n, h]], axis=0)).astype(bf16)

    def recur_u(ci):
        vb = vb_ref[0, chunk_rows(ci), :]
        zs = [z_ref[h] for h in heads]
        zbs = [z.astype(bf16) for z in zs]
        vbs = [vb[:, hsl[h]] for h in heads]
        ubs = [_dot(wb_ref[rd, ci, h], jnp.concatenate([zbs[h], vbs[h]], axis=0)).astype(bf16)
               for h in heads]
        return zs, zbs, vbs, ubs

    def recur_zy(ci, ctx):
        zs, zbs, vbs, ubs = ctx
        ys = []
        for h in heads:
            zy = _dot(zy_ref[rd, ci, h], jnp.concatenate([ubs[h], vbs[h], zbs[h]], axis=0))
            z_ref[h] = dcol_ref[rd, ci, h] * zs[h] + zy[0:c, :]
            ys.append(zy[c:2 * c, :])
        rows = chunk_rows(ci)
        ycs = [y - jnp.mean(y, axis=-1, keepdims=True) for y in ys]
        rstd = [lax.rsqrt(jnp.mean(yc * yc, axis=-1, keepdims=True) + GN_EPS) for yc in ycs]
        yn = jnp.concatenate(ycs, axis=1) * _head_bcast(rstd, low_half) * gnw_ref[...] + gnb_ref[...]
        o_ref[0, rows, :] = (yn + bonus_ref[0, rows, :]) * g_ref[0, rows, :]

    def group(cis):
        state = {}

        def products(n):
            state[n] = solve_products([cis[n]])

        def level(lvl, n):
            solve_level(state[n][0], lvl)

        def store(n):
            solve_store(state[n][0], state[n][1], [cis[n]])

        members = range(len(cis))
        solve = ([functools.partial(products, n) for n in members]
                 + [functools.partial(level, lvl, n) for lvl in range(6) for n in members]
                 + [functools.partial(store, n) for n in members])
        ctx = [None]

        def recur_first(ci):
            ctx[0] = recur_u(ci)

        def recur_second(ci):
            recur_zy(ci, ctx[0])

        recur = []
        for ci in cis:
            recur += [functools.partial(recur_first, ci), functools.partial(recur_second, ci)]
        assert len(recur) <= len(solve)
        for k, stage in enumerate(solve):
            if k < len(recur):
                recur[k]()
            stage()

    n_full = n_chunks // SCAN_GROUP

    def full_body(i, carry):
        group([SCAN_GROUP * i + k for k in range(SCAN_GROUP)])
        return carry

    lax.fori_loop(0, n_full, full_body, 0)
    if n_chunks % SCAN_GROUP:
        group(list(range(SCAN_GROUP * n_full, n_chunks)))


def _scan(at, rt, bt, kt, bh, kh, vb, bonus, g, egc, gn_w, gn_b):
    bsz, tp, _ = at.shape
    ts = TT_SCAN
    n_chunks = ts // CHUNK
    tiles_per_row = tp // ts
    n_tiles = bsz * tiles_per_row

    def cur_map(s):
        t = jnp.minimum(s, n_tiles - 1)
        return (t // tiles_per_row, t % tiles_per_row, 0)

    def lag_map(s):
        t = jnp.maximum(s - 1, 0)
        return (t // tiles_per_row, t % tiles_per_row, 0)

    cur = pl.BlockSpec((1, ts, RWKV_WIDTH), cur_map)
    lag = pl.BlockSpec((1, ts, RWKV_WIDTH), lag_map)
    ecur = pl.BlockSpec((1, n_chunks * SUBLANES, RWKV_WIDTH), cur_map)
    cspec = pl.BlockSpec((1, RWKV_WIDTH), lambda s: (0, 0))
    return pl.pallas_call(
        functools.partial(_scan_kernel, ts=ts, tiles_per_row=tiles_per_row),
        out_shape=jax.ShapeDtypeStruct((bsz, tp, RWKV_WIDTH), f32),
        grid=(n_tiles + 1,),
        in_specs=[cur] * 6 + [ecur] + [lag] * 3 + [cspec] * 2,
        out_specs=lag,
        scratch_shapes=[
            pltpu.VMEM((N_HEADS, HEAD_SIZE, HEAD_SIZE), f32),
            pltpu.VMEM((2, n_chunks, N_HEADS, CHUNK, 2 * CHUNK), bf16),
            pltpu.VMEM((2, n_chunks, N_HEADS, 2 * CHUNK, 3 * CHUNK), bf16),
            pltpu.VMEM((2, n_chunks, N_HEADS, HEAD_SIZE, HEAD_SIZE), f32),
        ],
        compiler_params=pltpu.CompilerParams(
            dimension_semantics=("arbitrary",), vmem_limit_bytes=VMEM_LIMIT),
        name="rwkv_scan",
    )(at, rt, bt, kt, bh, kh, egc, vb, bonus, g, gn_w, gn_b)


def _out_ffn_kernel(h_ref, pool_ref, rw_ref, wo_ref, g2_ref, wg_ref, wu_ref, wd_ref, gf_ref,
                    o_ref, *, final):
    load = (lambda ref: ref[0]) if final else (lambda ref: ref[...])
    mixed = (_dot(load(pool_ref).astype(bf16), wo_ref[0:POOL_WIDTH, :])
             + _dot(load(rw_ref).astype(bf16), wo_ref[POOL_WIDTH:, :]))
    h1 = load(h_ref) + mixed
    hn = h1 * lax.rsqrt(jnp.mean(h1 * h1, axis=-1, keepdims=True) + RMS_EPS) * g2_ref[...]
    hb = hn.astype(bf16)
    acc = h1
    start = 0
    for width in FF_BLKS:
        cs = slice(start, start + width)
        start += width
        gate = _dot(hb, wg_ref[:, cs])
        up = _dot(hb, wu_ref[:, cs])
        act = gate * _sigmoid(gate) * up
        acc = acc + _dot(act.astype(bf16), wd_ref[cs, :])
    if final:
        o_ref[0] = acc * lax.rsqrt(jnp.mean(acc * acc, axis=-1, keepdims=True) + RMS_EPS) * gf_ref[...]
    else:
        o_ref[...] = acc


def _out_ffn(h, pool, rw, w_out, g2, w_gate_up, w_down, g_final, layer, final, seq):
    bsz, tp, _ = h.shape
    tm = TM_FFN
    assert sum(FF_BLKS) == D_FF
    cspec = lambda shape: pl.BlockSpec(shape, lambda *_: (0, 0), pipeline_mode=pl.Buffered(1))
    stacked = lambda shape, half=0: pl.BlockSpec(
        (None,) + shape, lambda *_: (layer, 0, half), pipeline_mode=pl.Buffered(1))
    weights = [stacked((D_MODEL, D_MODEL)), cspec((1, D_MODEL)),
               stacked((D_MODEL, D_FF), 0), stacked((D_MODEL, D_FF), 1),
               stacked((D_FF, D_MODEL)), cspec((1, D_MODEL))]
    if final:
        assert seq % tm == 0 and N_META % SUBLANES == 0
        win = lambda width: pl.BlockSpec(
            (pl.Element(1), pl.Element(tm), pl.Element(width)),
            lambda b, i: (b, pl.multiple_of(N_META + i * tm, SUBLANES), 0))
        args = (h, pool, rw)
        grid = (bsz, seq // tm)
        out_shape = jax.ShapeDtypeStruct((bsz, seq, D_MODEL), f32)
        out_spec = pl.BlockSpec((1, tm, D_MODEL), lambda b, i: (b, i, 0))
        sem = ("arbitrary", "arbitrary")
    else:
        rows = bsz * tp
        win = lambda width: pl.BlockSpec((tm, width), lambda i: (i, 0))
        args = tuple(t.reshape(rows, t.shape[-1]) for t in (h, pool, rw))
        grid = (rows // tm,)
        out_shape = jax.ShapeDtypeStruct((rows, D_MODEL), f32)
        out_spec = win(D_MODEL)
        sem = ("arbitrary",)
    out = pl.pallas_call(
        functools.partial(_out_ffn_kernel, final=final),
        out_shape=out_shape,
        grid=grid,
        in_specs=[win(D_MODEL), win(POOL_WIDTH), win(RWKV_WIDTH)] + weights,
        out_specs=out_spec,
        compiler_params=pltpu.CompilerParams(dimension_semantics=sem, vmem_limit_bytes=VMEM_LIMIT),
        name="out_ffn_final" if final else "out_ffn",
    )(*args, w_out, g2, w_gate_up, w_gate_up, w_down, g_final)
    return out if final else out.reshape(bsz, tp, D_MODEL)


def _row(p):
    return p.reshape(1, -1).astype(f32)


def _pad_rows(w, before, total):
    return jnp.pad(w, ((before, total - before - w.shape[0]), (0, 0)))


def kernel(x, meta_tokens, ln1_g, w_in, mu_shift, pool_w, pool_scale, w0, w_lora_up, a0, a_lora_up, g_lora_up, k_k, k_a, r_k, gn_w, gn_b, w_in_vres, mu_vres, v0, v_lora_up, w_out, ln2_g, w_gate_up, w_down, final_g):
    bsz, seq, _ = x.shape
    depth = w_in.shape[0]
    t_real = N_META + seq
    tp = SEQ_PAD_TO
    assert t_real <= tp and tp % TT_MIX == 0 and tp % TT_SCAN == 0 and (bsz * tp) % TM_FFN == 0
    meta = jnp.broadcast_to(meta_tokens.astype(x.dtype)[None], (bsz, N_META, D_MODEL))
    h = jnp.concatenate([meta, x, jnp.zeros((bsz, tp - t_real, D_MODEL), x.dtype)], axis=1)

    w_out_b, w_gate_up_b, w_down_b = (w.astype(bf16) for w in (w_out, w_gate_up, w_down))
    v_first = None
    for l in range(depth):
        n_in = w_in.shape[2]
        w_cols = [w_in[l]]
        mu_cols = [jnp.zeros((POOL_WIDTH,), f32), mu_shift[l].astype(f32)]
        used = n_in
        if l > 0:
            w_cols.append(w_in_vres[l - 1])
            mu_cols.append(mu_vres[l - 1].astype(f32))
            used += MV_LORA
        w_comb = jnp.pad(jnp.concatenate(w_cols, axis=1), ((0, 0), (0, N_COMB - used))).astype(bf16)
        mu = jnp.pad(jnp.concatenate(mu_cols), (0, N_COMB - used)).reshape(1, N_COMB)
        wup = _pad_rows(w_lora_up[l], 0, LO1_W).astype(bf16)
        aup = _pad_rows(a_lora_up[l], DECAY_LORA, LO1_W).astype(bf16)
        gup = _pad_rows(g_lora_up[l], 0, LO2_W).astype(bf16)
        vres = None
        if l > 0:
            vup = _pad_rows(v_lora_up[l - 1], GATE_LORA, LO2_W).astype(bf16)
            vres = (_row(v0[l - 1]), vup, v_first)
        outs = _mix_in(
            h, _row(ln1_g[l]), w_comb, mu, pool_w[l].astype(bf16), _row(pool_scale[l]),
            _row(w0[l]), wup, _row(a0[l]), aup, gup, _row(k_k[l]), _row(k_a[l]), _row(r_k[l]), vres)
        pool_o, scan_args = outs[0], outs[1:11]
        if l == 0:
            v_first = outs[11]
        rw = _scan(*scan_args, _row(gn_w[l]), _row(gn_b[l]))
        h = _out_ffn(h, pool_o, rw, w_out_b, _row(ln2_g[l]), w_gate_up_b, w_down_b, _row(final_g),
                     layer=l, final=(l == depth - 1), seq=seq)
    return h
```

```python
import functools

import jax
import jax.numpy as jnp
from jax import lax
from jax.experimental import pallas as pl
from jax.experimental.pallas import tpu as pltpu

f32 = jnp.float32
bf16 = jnp.bfloat16

D_MODEL = 1024
N_META = 16
POOL_WIDTH = 512
POOL_WINDOWS = (2, 4, 8, 16)
POOL_GROUP = 128
RWKV_WIDTH = 512
HEAD_SIZE = 64
N_HEADS = 8
DECAY_LORA = 64
AAA_LORA = 64
MV_LORA = 32
GATE_LORA = 160
D_FF = 2816
RMS_EPS = 1e-6
GN_EPS = 64e-5
L2_EPS = 1e-12

C_R = POOL_WIDTH
C_K = C_R + RWKV_WIDTH
C_V = C_K + RWKV_WIDTH
C_LO1 = C_V + RWKV_WIDTH
C_LO2 = C_LO1 + DECAY_LORA + AAA_LORA
LO1_W = 128
LO2_W = 256
N_COMB = C_LO2 + LO2_W

CHUNK = 64
SEQ_PAD_TO = CHUNK * 33
HALO = 16
SUBLANES = 8
TT_MIX = 3 * CHUNK
TT_SCAN = 11 * CHUNK
SCAN_GROUP = 6
TM_FFN = 512
FF_BLKS = (768, 768, 768, 512)
VMEM_LIMIT = 56 * 1024 * 1024

_HEAD_SLICES = [slice(h * HEAD_SIZE, (h + 1) * HEAD_SIZE) for h in range(N_HEADS)]


def _dot(a, b):
    return jnp.dot(a, b, preferred_element_type=f32)


def _dot_nt(a, b):
    return lax.dot_general(a, b, (((1,), (1,)), ((), ())), preferred_element_type=f32)


def _sigmoid(x):
    return 1.0 / (1.0 + jnp.exp(-x))


def _head_sums(x):
    return [jnp.sum(x[:, hs], axis=-1, keepdims=True) for hs in _HEAD_SLICES]


def _head_bcast(cols, low_half):
    return jnp.concatenate(
        [jnp.where(low_half, cols[2 * p], cols[2 * p + 1]) for p in range(N_HEADS // 2)], axis=1)


def _mix_in_kernel(*refs, has_vres, tt, n_tiles, tiles_per_row):
    if has_vres:
        (h_ref, g1_ref, w_ref, mu_ref, pw_ref, ps_ref, w0_ref, wup_ref, a0_ref, aup_ref,
         gup_ref, kk_ref, ka_ref, rk_ref, v0_ref, vup_ref, vf_ref,
         pool_ref, at_ref, rt_ref, bt_ref, kt_ref, bh_ref, kh_ref, vb_ref, bonus_ref, g_ref,
         egc_ref, e_ref, stash_ref) = refs
        vout_ref = None
    else:
        (h_ref, g1_ref, w_ref, mu_ref, pw_ref, ps_ref, w0_ref, wup_ref, a0_ref, aup_ref,
         gup_ref, kk_ref, ka_ref, rk_ref,
         pool_ref, at_ref, rt_ref, bt_ref, kt_ref, bh_ref, kh_ref, vb_ref, bonus_ref, g_ref,
         egc_ref, vout_ref, e_ref, stash_ref) = refs
    s = pl.program_id(0)
    j = jnp.minimum(s, n_tiles - 1) % tiles_per_row
    c = CHUNK

    @pl.when(s == 0)
    def _():
        stash_ref[...] = jnp.zeros(stash_ref.shape, f32)

    @pl.when(j == 0)
    def _():
        e_ref[0:HALO, :] = jnp.zeros((HALO, N_COMB), f32)

    prev = stash_ref.at[(s + 1) % 2]
    r, kraw, v, lw, a_sig = prev[0], prev[1], prev[2], prev[3], prev[4]
    row = lax.broadcasted_iota(jnp.int32, (c, c), 0)
    col = lax.broadcasted_iota(jnp.int32, (c, c), 1)
    tri_incl = (row >= col).astype(bf16)
    lw_hi = lw.astype(bf16)
    lw_lo = (lw - lw_hi.astype(f32)).astype(bf16)
    g_parts, egc_parts = [], []
    for ci in range(tt // c):
        rows = slice(ci * c, (ci + 1) * c)
        g_ci = _dot(tri_incl, lw_hi[rows]) + _dot(tri_incl, lw_lo[rows])
        egc = jnp.exp(g_ci[c - 1:c, :])
        egc_ref[0, ci * SUBLANES:(ci + 1) * SUBLANES, :] = jnp.broadcast_to(egc, (SUBLANES, RWKV_WIDTH))
        g_parts.append(g_ci)
        egc_parts.append(jnp.broadcast_to(egc, (c, RWKV_WIDTH)))
    g = jnp.concatenate(g_parts, axis=0)
    egc_rows = jnp.concatenate(egc_parts, axis=0)
    low_half = lax.broadcasted_iota(jnp.int32, (tt, 2 * HEAD_SIZE), 1) < HEAD_SIZE
    kkx = kraw * kk_ref[...]
    inv_norm = [lax.rsqrt(jnp.maximum(ss, L2_EPS * L2_EPS)) for ss in _head_sums(kkx * kkx)]
    kk = kkx * _head_bcast(inv_norm, low_half)
    kf = kraw * (1.0 + (a_sig - 1.0) * ka_ref[...])
    eng = jnp.exp(-g)
    kt = kf * eng
    bt = kk * a_sig * eng
    rt_ref[0] = (r * jnp.exp(g)).astype(bf16)
    at_ref[0] = (-(kk * jnp.exp(g - lw))).astype(bf16)
    kt_ref[0] = kt.astype(bf16)
    bt_ref[0] = bt.astype(bf16)
    kh_ref[0] = (kt * egc_rows).astype(bf16)
    bh_ref[0] = (bt * egc_rows).astype(bf16)
    vb_ref[0] = v.astype(bf16)
    bonus_ref[0] = _head_bcast(_head_sums(r * kf * rk_ref[...]), low_half) * v

    x = h_ref[0]
    hn = x * lax.rsqrt(jnp.mean(x * x, axis=-1, keepdims=True) + RMS_EPS) * g1_ref[...]
    e_ref[HALO:HALO + tt, :] = _dot(hn.astype(bf16), w_ref[...])

    pos = j * tt + lax.broadcasted_iota(jnp.int32, (tt, 1), 0)
    for gi, win in enumerate(POOL_WINDOWS):
        c0 = gi * POOL_GROUP
        u_ext = e_ref[:, c0:c0 + POOL_GROUP]
        acc = u_ext
        d = 1
        while d < win:
            acc = acc + pltpu.roll(acc, d, axis=0)
            d *= 2
        cnt = jnp.minimum(pos + 1, win).astype(f32)
        diff = acc[HALO:, :] / cnt - u_ext[HALO:, :]
        y = _dot(diff.astype(bf16), pw_ref[gi])
        pool_ref[0, :, c0:c0 + POOL_GROUP] = y * ps_ref[:, c0:c0 + POOL_GROUP]

    def shifted(c0, c1):
        cur = e_ref[HALO:HALO + tt, c0:c1]
        before = e_ref[HALO - 1:HALO - 1 + tt, c0:c1]
        return cur + (before - cur) * mu_ref[:, c0:c1]

    nxt = stash_ref.at[s % 2]
    nxt[0] = shifted(C_R, C_K)
    nxt[1] = shifted(C_K, C_V)
    v_new = shifted(C_V, C_LO1)
    lo1 = shifted(C_LO1, C_LO2)
    lo2 = shifted(C_LO2, N_COMB)
    zw = w0_ref[...] + _dot(jnp.tanh(lo1).astype(bf16), wup_ref[...])
    nxt[3] = (-jnp.exp(f32(-0.5))) * _sigmoid(zw)
    nxt[4] = _sigmoid(a0_ref[...] + _dot(lo1.astype(bf16), aup_ref[...]))
    g_ref[0] = _dot(_sigmoid(lo2).astype(bf16), gup_ref[...])
    if has_vres:
        v_mix = _sigmoid(v0_ref[...] + _dot(lo2.astype(bf16), vup_ref[...]))
        v_new = v_new + (vf_ref[0] - v_new) * v_mix
    else:
        vout_ref[0] = v_new
    nxt[2] = v_new

    e_ref[0:HALO, :] = jnp.where(s < n_tiles - 1, e_ref[tt:tt + HALO, :], e_ref[0:HALO, :])


def _mix_in(h, g1, w_comb, mu, pool_w, pool_scale, w0, wup, a0, aup, gup, k_k, k_a, r_k, vres):
    bsz, tp, _ = h.shape
    tt = TT_MIX
    tiles_per_row = tp // tt
    n_tiles = bsz * tiles_per_row
    egc_rows = tt // CHUNK * SUBLANES
    has_vres = vres is not None

    def cur_map(s):
        t = jnp.minimum(s, n_tiles - 1)
        return (t // tiles_per_row, t % tiles_per_row, 0)

    def lag_map(s):
        t = jnp.maximum(s - 1, 0)
        return (t // tiles_per_row, t % tiles_per_row, 0)

    cur = lambda width: pl.BlockSpec((1, tt, width), cur_map)
    lag = lambda width: pl.BlockSpec((1, tt, width), lag_map)
    cspec = lambda shape: pl.BlockSpec(shape, lambda s: (0,) * len(shape), pipeline_mode=pl.Buffered(1))
    in_specs = [
        cur(D_MODEL), cspec((1, D_MODEL)), cspec((D_MODEL, N_COMB)), cspec((1, N_COMB)),
        cspec((len(POOL_WINDOWS), POOL_GROUP, POOL_GROUP)),
        cspec((1, POOL_WIDTH)), cspec((1, RWKV_WIDTH)), cspec((LO1_W, RWKV_WIDTH)),
        cspec((1, RWKV_WIDTH)), cspec((LO1_W, RWKV_WIDTH)), cspec((LO2_W, RWKV_WIDTH)),
        cspec((1, RWKV_WIDTH)), cspec((1, RWKV_WIDTH)), cspec((1, RWKV_WIDTH)),
    ]
    args = [h, g1, w_comb, mu, pool_w, pool_scale, w0, wup, a0, aup, gup, k_k, k_a, r_k]
    wide = lambda dtype: jax.ShapeDtypeStruct((bsz, tp, RWKV_WIDTH), dtype)
    out_shape = [wide(f32)] + [wide(bf16)] * 7 + [wide(f32), wide(f32),
                 jax.ShapeDtypeStruct((bsz, tp // CHUNK * SUBLANES, RWKV_WIDTH), f32)]
    out_specs = [cur(RWKV_WIDTH)] + [lag(RWKV_WIDTH)] * 8 + [
        cur(RWKV_WIDTH), pl.BlockSpec((1, egc_rows, RWKV_WIDTH), lag_map)]
    if has_vres:
        v0, vup, v_first = vres
        in_specs += [cspec((1, RWKV_WIDTH)), cspec((LO2_W, RWKV_WIDTH)), cur(RWKV_WIDTH)]
        args += [v0, vup, v_first]
    else:
        out_shape.append(wide(f32))
        out_specs.append(cur(RWKV_WIDTH))
    return pl.pallas_call(
        functools.partial(_mix_in_kernel, has_vres=has_vres, tt=tt, n_tiles=n_tiles,
                          tiles_per_row=tiles_per_row),
        out_shape=tuple(out_shape),
        grid=(n_tiles + 1,),
        in_specs=in_specs,
        out_specs=tuple(out_specs),
        scratch_shapes=[pltpu.VMEM((HALO + tt, N_COMB), f32),
                        pltpu.VMEM((2, 5, tt, RWKV_WIDTH), f32)],
        compiler_params=pltpu.CompilerParams(
            dimension_semantics=("arbitrary",), vmem_limit_bytes=VMEM_LIMIT),
        name="mix_in_vres" if has_vres else "mix_in",
    )(*args)


def _scan_kernel(at_ref, rt_ref, bt_ref, kt_ref, bh_ref, kh_ref, egc_ref,
                 vb_ref, bonus_ref, g_ref, gnw_ref, gnb_ref,
                 o_ref, z_ref, wb_ref, zy_ref, dcol_ref, *, ts, tiles_per_row):
    s = pl.program_id(0)
    wr = s % 2
    rd = (s + 1) % 2

    @pl.when(s == 0)
    def _():
        wb_ref[...] = jnp.zeros(wb_ref.shape, bf16)
        zy_ref[...] = jnp.zeros(zy_ref.shape, bf16)
        dcol_ref[...] = jnp.zeros(dcol_ref.shape, f32)

    @pl.when((s + tiles_per_row - 1) % tiles_per_row == 0)
    def _():
        z_ref[...] = jnp.zeros(z_ref.shape, f32)

    c = CHUNK
    n_chunks = ts // c
    row2 = lax.broadcasted_iota(jnp.int32, (2 * c, 2 * c), 0)
    col2 = lax.broadcasted_iota(jnp.int32, (2 * c, 2 * c), 1)
    rm, cm = row2 & (c - 1), col2 & (c - 1)
    a_mask = (rm > cm) | ((row2 >= c) & (rm == cm))
    rowv = lax.broadcasted_iota(jnp.int32, (c, 2 * c), 0)
    colv = lax.broadcasted_iota(jnp.int32, (c, 2 * c), 1)
    low_half = colv < c
    heads = range(N_HEADS)
    hsl = _HEAD_SLICES

    def chunk_rows(ci):
        return pl.ds(pl.multiple_of(ci * c, c), c)

    def solve_products(cis):
        ws, xs = {}, {}
        pair_lo = lax.broadcasted_iota(jnp.int32, (2 * c, 2 * c), 1) < c
        zeros_cc = jnp.zeros((c, c), bf16)
        eye_hi = (colv == rowv + c).astype(f32)
        for n, ci in enumerate(cis):
            rows = chunk_rows(ci)
            at, rt, bt, kt = at_ref[0, rows, :], rt_ref[0, rows, :], bt_ref[0, rows, :], kt_ref[0, rows, :]
            bh, kh = bh_ref[0, rows, :], kh_ref[0, rows, :]
            egc = egc_ref[0, pl.ds(pl.multiple_of(ci * SUBLANES, SUBLANES), SUBLANES), :][0:1, :]
            for p in range(N_HEADS // 2):
                lanes = slice(2 * c * p, 2 * c * (p + 1))
                la = jnp.concatenate([at[:, lanes], rt[:, lanes]], axis=0)
                ra = jnp.concatenate([bt[:, lanes], kt[:, lanes]], axis=0)
                zero = jnp.zeros_like(la)
                am2 = _dot_nt(jnp.concatenate([jnp.where(pair_lo, la, zero),
                                               jnp.where(pair_lo, zero, la)], axis=0), ra)
                hat2 = jnp.transpose(jnp.concatenate([bh[:, lanes], kh[:, lanes]], axis=0))
                dcol = jnp.transpose(jnp.broadcast_to(egc[:, lanes], (c, 2 * c)))
                at_pair = at[:, lanes].astype(f32)
                at_low = [at_pair, pltpu.roll(at_pair, c, axis=1)]
                for q in range(2):
                    h = 2 * p + q
                    amat = jnp.where(a_mask, am2[2 * c * q:2 * c * (q + 1), :], 0.0)
                    top = amat[0:c, :]
                    ws[n, h] = jnp.where(low_half, top, eye_hi)
                    xs[n, h] = jnp.where(low_half, at_low[q], top).astype(bf16)
                    zy_ref[wr, ci, h] = jnp.concatenate(
                        [jnp.concatenate([hat2[c * q:c * (q + 1), :], zeros_cc], axis=1),
                         jnp.concatenate([amat[c:2 * c, :].astype(bf16), rt[:, hsl[h]]], axis=1)], axis=0)
                    dcol_ref[wr, ci, h] = dcol[c * q:c * (q + 1), :]
        return ws, xs

    def solve_level(ws, lvl):
        same = low_half & ((rowv >> (lvl + 1)) == (colv >> (lvl + 1)))
        for key in ws:
            w = ws[key]
            dmat = jnp.where(same, w, 0.0)[:, 0:c]
            ws[key] = jnp.where(same, 0.0, w) + _dot(dmat.astype(bf16), w.astype(bf16))

    def solve_store(ws, xs, cis):
        zeros_rows = jnp.zeros((c, 2 * c), bf16)
        for (n, h), w in ws.items():
            wb_ref[wr, cis[n], h] = _dot(
                w.astype(bf16), jnp.concatenate([zeros_rows, xs[n, h]], axis=0)).astype(bf16)

    def recur_u(ci):
        vb = vb_ref[0, chunk_rows(ci), :]
        zs = [z_ref[h] for h in heads]
        zbs = [z.astype(bf16) for z in zs]
        vbs = [vb[:, hsl[h]] for h in heads]
        ubs = [_dot(wb_ref[rd, ci, h], jnp.concatenate([zbs[h], vbs[h]], axis=0)).astype(bf16)
               for h in heads]
        return zs, zbs, vbs, ubs

    def recur_zy(ci, ctx):
        zs, zbs, vbs, ubs = ctx
        ys = []
        for h in heads:
            zy = _dot(zy_ref[rd, ci, h], jnp.concatenate([ubs[h], vbs[h], zbs[h]], axis=0))
            z_ref[h] = dcol_ref[rd, ci, h] * zs[h] + zy[0:c, :]
            ys.append(zy[c:2 * c, :])
        rows = chunk_rows(ci)
        ycs = [y - jnp.mean(y, axis=-1, keepdims=True) for y in ys]
        rstd = [lax.rsqrt(jnp.mean(yc * yc, axis=-1, keepdims=True) + GN_EPS) for yc in ycs]
        yn = jnp.concatenate(ycs, axis=1) * _head_bcast(rstd, low_half) * gnw_ref[...] + gnb_ref[...]
        o_ref[0, rows, :] = (yn + bonus_ref[0, rows, :]) * g_ref[0, rows, :]

    def group(cis):
        state = {}

        def products(n):
            state[n] = solve_products([cis[n]])

        def level(lvl, n):
            solve_level(state[n][0], lvl)

        def store(n):
            solve_store(state[n][0], state[n][1], [cis[n]])

        members = range(len(cis))
        solve = ([functools.partial(products, n) for n in members]
                 + [functools.partial(level, lvl, n) for lvl in range(6) for n in members]
                 + [functools.partial(store, n) for n in members])
        ctx = [None]

        def recur_first(ci):
            ctx[0] = recur_u(ci)

        def recur_second(ci):
            recur_zy(ci, ctx[0])

        recur = []
        for ci in cis:
            recur += [functools.partial(recur_first, ci), functools.partial(recur_second, ci)]
        assert len(recur) <= len(solve)
        for k, stage in enumerate(solve):
            if k < len(recur):
                recur[k]()
            stage()

    n_full = n_chunks // SCAN_GROUP

    def full_body(i, carry):
        group([SCAN_GROUP * i + k for k in range(SCAN_GROUP)])
        return carry

    lax.fori_loop(0, n_full, full_body, 0)
    if n_chunks % SCAN_GROUP:
        group(list(range(SCAN_GROUP * n_full, n_chunks)))


def _scan(at, rt, bt, kt, bh, kh, vb, bonus, g, egc, gn_w, gn_b):
    bsz, tp, _ = at.shape
    ts = TT_SCAN
    n_chunks = ts // CHUNK
    tiles_per_row = tp // ts
    n_tiles = bsz * tiles_per_row

    def cur_map(s):
        t = jnp.minimum(s, n_tiles - 1)
        return (t // tiles_per_row, t % tiles_per_row, 0)

    def lag_map(s):
        t = jnp.maximum(s - 1, 0)
        return (t // tiles_per_row, t % tiles_per_row, 0)

    cur = pl.BlockSpec((1, ts, RWKV_WIDTH), cur_map)
    lag = pl.BlockSpec((1, ts, RWKV_WIDTH), lag_map)
    ecur = pl.BlockSpec((1, n_chunks * SUBLANES, RWKV_WIDTH), cur_map)
    cspec = pl.BlockSpec((1, RWKV_WIDTH), lambda s: (0, 0))
    return pl.pallas_call(
        functools.partial(_scan_kernel, ts=ts, tiles_per_row=tiles_per_row),
        out_shape=jax.ShapeDtypeStruct((bsz, tp, RWKV_WIDTH), f32),
        grid=(n_tiles + 1,),
        in_specs=[cur] * 6 + [ecur] + [lag] * 3 + [cspec] * 2,
        out_specs=lag,
        scratch_shapes=[
            pltpu.VMEM((N_HEADS, HEAD_SIZE, HEAD_SIZE), f32),
            pltpu.VMEM((2, n_chunks, N_HEADS, CHUNK, 2 * CHUNK), bf16),
            pltpu.VMEM((2, n_chunks, N_HEADS, 2 * CHUNK, 3 * CHUNK), bf16),
            pltpu.VMEM((2, n_chunks, N_HEADS, HEAD_SIZE, HEAD_SIZE), f32),
        ],
        compiler_params=pltpu.CompilerParams(
            dimension_semantics=("arbitrary",), vmem_limit_bytes=VMEM_LIMIT),
        name="rwkv_scan",
    )(at, rt, bt, kt, bh, kh, egc, vb, bonus, g, gn_w, gn_b)


def _out_ffn_kernel(h_ref, pool_ref, rw_ref, wo_ref, g2_ref, wg_ref, wu_ref, wd_ref, gf_ref,
                    o_ref, *, final):
    load = (lambda ref: ref[0]) if final else (lambda ref: ref[...])
    mixed = (_dot(load(pool_ref).astype(bf16), wo_ref[0:POOL_WIDTH, :])
             + _dot(load(rw_ref).astype(bf16), wo_ref[POOL_WIDTH:, :]))
    h1 = load(h_ref) + mixed
    hn = h1 * lax.rsqrt(jnp.mean(h1 * h1, axis=-1, keepdims=True) + RMS_EPS) * g2_ref[...]
    hb = hn.astype(bf16)
    acc = h1
    start = 0
    for width in FF_BLKS:
        cs = slice(start, start + width)
        start += width
        gate = _dot(hb, wg_ref[:, cs])
        up = _dot(hb, wu_ref[:, cs])
        act = gate * _sigmoid(gate) * up
        acc = acc + _dot(act.astype(bf16), wd_ref[cs, :])
    if final:
        o_ref[0] = acc * lax.rsqrt(jnp.mean(acc * acc, axis=-1, keepdims=True) + RMS_EPS) * gf_ref[...]
    else:
        o_ref[...] = acc


def _out_ffn(h, pool, rw, w_out, g2, w_gate_up, w_down, g_final, layer, final, seq):
    bsz, tp, _ = h.shape
    tm = TM_FFN
    assert sum(FF_BLKS) == D_FF
    cspec = lambda shape: pl.BlockSpec(shape, lambda *_: (0, 0), pipeline_mode=pl.Buffered(1))
    stacked = lambda shape, half=0: pl.BlockSpec(
        (None,) + shape, lambda *_: (layer, 0, half), pipeline_mode=pl.Buffered(1))
    weights = [stacked((D_MODEL, D_MODEL)), cspec((1, D_MODEL)),
               stacked((D_MODEL, D_FF), 0), stacked((D_MODEL, D_FF), 1),
               stacked((D_FF, D_MODEL)), cspec((1, D_MODEL))]
    if final:
        assert seq % tm == 0 and N_META % SUBLANES == 0
        win = lambda width: pl.BlockSpec(
            (pl.Element(1), pl.Element(tm), pl.Element(width)),
            lambda b, i: (b, pl.multiple_of(N_META + i * tm, SUBLANES), 0))
        args = (h, pool, rw)
        grid = (bsz, seq // tm)
        out_shape = jax.ShapeDtypeStruct((bsz, seq, D_MODEL), f32)
        out_spec = pl.BlockSpec((1, tm, D_MODEL), lambda b, i: (b, i, 0))
        sem = ("arbitrary", "arbitrary")
    else:
        rows = bsz * tp
        win = lambda width: pl.BlockSpec((tm, width), lambda i: (i, 0))
        args = tuple(t.reshape(rows, t.shape[-1]) for t in (h, pool, rw))
        grid = (rows // tm,)
        out_shape = jax.ShapeDtypeStruct((rows, D_MODEL), f32)
        out_spec = win(D_MODEL)
        sem = ("arbitrary",)
    out = pl.pallas_call(
        functools.partial(_out_ffn_kernel, final=final),
        out_shape=out_shape,
        grid=grid,
        in_specs=[win(D_MODEL), win(POOL_WIDTH), win(RWKV_WIDTH)] + weights,
        out_specs=out_spec,
        compiler_params=pltpu.CompilerParams(dimension_semantics=sem, vmem_limit_bytes=VMEM_LIMIT),
        name="out_ffn_final" if final else "out_ffn",
    )(*args, w_out, g2, w_gate_up, w_gate_up, w_down, g_final)
    return out if final else out.reshape(bsz, tp, D_MODEL)


def _row(p):
    return p.reshape(1, -1).astype(f32)


def _pad_rows(w, before, total):
    return jnp.pad(w, ((before, total - before - w.shape[0]), (0, 0)))


def kernel(x, meta_tokens, ln1_g, w_in, mu_shift, pool_w, pool_scale, w0, w_lora_up, a0, a_lora_up, g_lora_up, k_k, k_a, r_k, gn_w, gn_b, w_in_vres, mu_vres, v0, v_lora_up, w_out, ln2_g, w_gate_up, w_down, final_g):
    bsz, seq, _ = x.shape
    depth = w_in.shape[0]
    t_real = N_META + seq
    tp = SEQ_PAD_TO
    assert t_real <= tp and tp % TT_MIX == 0 and tp % TT_SCAN == 0 and (bsz * tp) % TM_FFN == 0
    meta = jnp.broadcast_to(meta_tokens.astype(x.dtype)[None], (bsz, N_META, D_MODEL))
    h = jnp.concatenate([meta, x, jnp.zeros((bsz, tp - t_real, D_MODEL), x.dtype)], axis=1)

    w_out_b, w_gate_up_b, w_down_b = (w.astype(bf16) for w in (w_out, w_gate_up, w_down))
    v_first = None
    for l in range(depth):
        n_in = w_in.shape[2]
        w_cols = [w_in[l]]
        mu_cols = [jnp.zeros((POOL_WIDTH,), f32), mu_shift[l].astype(f32)]
        used = n_in
        if l > 0:
            w_cols.append(w_in_vres[l - 1])
            mu_cols.append(mu_vres[l - 1].astype(f32))
            used += MV_LORA
        w_comb = jnp.pad(jnp.concatenate(w_cols, axis=1), ((0, 0), (0, N_COMB - used))).astype(bf16)
        mu = jnp.pad(jnp.concatenate(mu_cols), (0, N_COMB - used)).reshape(1, N_COMB)
        wup = _pad_rows(w_lora_up[l], 0, LO1_W).astype(bf16)
        aup = _pad_rows(a_lora_up[l], DECAY_LORA, LO1_W).astype(bf16)
        gup = _pad_rows(g_lora_up[l], 0, LO2_W).astype(bf16)
        vres = None
        if l > 0:
            vup = _pad_rows(v_lora_up[l - 1], GATE_LORA, LO2_W).astype(bf16)
            vres = (_row(v0[l - 1]), vup, v_first)
        outs = _mix_in(
            h, _row(ln1_g[l]), w_comb, mu, pool_w[l].astype(bf16), _row(pool_scale[l]),
            _row(w0[l]), wup, _row(a0[l]), aup, gup, _row(k_k[l]), _row(k_a[l]), _row(r_k[l]), vres)
        pool_o, scan_args = outs[0], outs[1:11]
        if l == 0:
            v_first = outs[11]
        rw = _scan(*scan_args, _row(gn_w[l]), _row(gn_b[l]))
        h = _out_ffn(h, pool_o, rw, w_out_b, _row(ln2_g[l]), w_gate_up_b, w_down_b, _row(final_g),
                     layer=l, final=(l == depth - 1), seq=seq)
    return h
```

```python
import functools

import jax
import jax.numpy as jnp
from jax import lax
from jax.experimental import pallas as pl
from jax.experimental.pallas import tpu as pltpu

f32 = jnp.float32
bf16 = jnp.bfloat16

D_MODEL = 1024
N_META = 16
POOL_WIDTH = 512
POOL_WINDOWS = (2, 4, 8, 16)
POOL_GROUP = 128
RWKV_WIDTH = 512
HEAD_SIZE = 64
N_HEADS = 8
DECAY_LORA = 64
AAA_LORA = 64
MV_LORA = 32
GATE_LORA = 160
D_FF = 2816
RMS_EPS = 1e-6
GN_EPS = 64e-5
L2_EPS = 1e-12

C_R = POOL_WIDTH
C_K = C_R + RWKV_WIDTH
C_V = C_K + RWKV_WIDTH
C_LO1 = C_V + RWKV_WIDTH
C_LO2 = C_LO1 + DECAY_LORA + AAA_LORA
LO1_W = 128
LO2_W = 256
N_COMB = C_LO2 + LO2_W

CHUNK = 64
SEQ_PAD_TO = CHUNK * 33
HALO = 16
SUBLANES = 8
TT_MIX = 3 * CHUNK
TT_SCAN = 11 * CHUNK
SCAN_GROUP = 11
TM_FFN = 512
FF_BLKS = (768, 768, 768, 512)
VMEM_LIMIT = 56 * 1024 * 1024

_HEAD_SLICES = [slice(h * HEAD_SIZE, (h + 1) * HEAD_SIZE) for h in range(N_HEADS)]


def _dot(a, b):
    return jnp.dot(a, b, preferred_element_type=f32)


def _dot_nt(a, b):
    return lax.dot_general(a, b, (((1,), (1,)), ((), ())), preferred_element_type=f32)


def _sigmoid(x):
    return 1.0 / (1.0 + jnp.exp(-x))


def _head_sums(x):
    return [jnp.sum(x[:, hs], axis=-1, keepdims=True) for hs in _HEAD_SLICES]


def _head_bcast(cols, low_half):
    return jnp.concatenate(
        [jnp.where(low_half, cols[2 * p], cols[2 * p + 1]) for p in range(N_HEADS // 2)], axis=1)


def _mix_in_kernel(*refs, has_vres, tt, n_tiles, tiles_per_row):
    if has_vres:
        (h_ref, g1_ref, w_ref, mu_ref, pw_ref, ps_ref, w0_ref, wup_ref, a0_ref, aup_ref,
         gup_ref, kk_ref, ka_ref, rk_ref, v0_ref, vup_ref, vf_ref,
         pool_ref, at_ref, rt_ref, bt_ref, kt_ref, bh_ref, kh_ref, vb_ref, bonus_ref, g_ref,
         egc_ref, e_ref, stash_ref) = refs
        vout_ref = None
    else:
        (h_ref, g1_ref, w_ref, mu_ref, pw_ref, ps_ref, w0_ref, wup_ref, a0_ref, aup_ref,
         gup_ref, kk_ref, ka_ref, rk_ref,
         pool_ref, at_ref, rt_ref, bt_ref, kt_ref, bh_ref, kh_ref, vb_ref, bonus_ref, g_ref,
         egc_ref, vout_ref, e_ref, stash_ref) = refs
    s = pl.program_id(0)
    j = jnp.minimum(s, n_tiles - 1) % tiles_per_row
    c = CHUNK

    @pl.when(s == 0)
    def _():
        stash_ref[...] = jnp.zeros(stash_ref.shape, f32)

    @pl.when(j == 0)
    def _():
        e_ref[0:HALO, :] = jnp.zeros((HALO, N_COMB), f32)

    prev = stash_ref.at[(s + 1) % 2]
    r, kraw, v, lw, a_sig = prev[0], prev[1], prev[2], prev[3], prev[4]
    row = lax.broadcasted_iota(jnp.int32, (c, c), 0)
    col = lax.broadcasted_iota(jnp.int32, (c, c), 1)
    tri_incl = (row >= col).astype(bf16)
    lw_hi = lw.astype(bf16)
    lw_lo = (lw - lw_hi.astype(f32)).astype(bf16)
    g_parts, egc_parts = [], []
    for ci in range(tt // c):
        rows = slice(ci * c, (ci + 1) * c)
        g_ci = _dot(tri_incl, lw_hi[rows]) + _dot(tri_incl, lw_lo[rows])
        egc = jnp.exp(g_ci[c - 1:c, :])
        egc_ref[0, ci * SUBLANES:(ci + 1) * SUBLANES, :] = jnp.broadcast_to(egc, (SUBLANES, RWKV_WIDTH))
        g_parts.append(g_ci)
        egc_parts.append(jnp.broadcast_to(egc, (c, RWKV_WIDTH)))
    g = jnp.concatenate(g_parts, axis=0)
    egc_rows = jnp.concatenate(egc_parts, axis=0)
    low_half = lax.broadcasted_iota(jnp.int32, (tt, 2 * HEAD_SIZE), 1) < HEAD_SIZE
    kkx = kraw * kk_ref[...]
    inv_norm = [lax.rsqrt(jnp.maximum(ss, L2_EPS * L2_EPS)) for ss in _head_sums(kkx * kkx)]
    kk = kkx * _head_bcast(inv_norm, low_half)
    kf = kraw * (1.0 + (a_sig - 1.0) * ka_ref[...])
    eng = jnp.exp(-g)
    kt = kf * eng
    bt = kk * a_sig * eng
    rt_ref[0] = (r * jnp.exp(g)).astype(bf16)
    at_ref[0] = (-(kk * jnp.exp(g - lw))).astype(bf16)
    kt_ref[0] = kt.astype(bf16)
    bt_ref[0] = bt.astype(bf16)
    kh_ref[0] = (kt * egc_rows).astype(bf16)
    bh_ref[0] = (bt * egc_rows).astype(bf16)
    vb_ref[0] = v.astype(bf16)
    bonus_ref[0] = _head_bcast(_head_sums(r * kf * rk_ref[...]), low_half) * v

    x = h_ref[0]
    hn = x * lax.rsqrt(jnp.mean(x * x, axis=-1, keepdims=True) + RMS_EPS) * g1_ref[...]
    e_ref[HALO:HALO + tt, :] = _dot(hn.astype(bf16), w_ref[...])

    pos = j * tt + lax.broadcasted_iota(jnp.int32, (tt, 1), 0)
    for gi, win in enumerate(POOL_WINDOWS):
        c0 = gi * POOL_GROUP
        u_ext = e_ref[:, c0:c0 + POOL_GROUP]
        acc = u_ext
        d = 1
        while d < win:
            acc = acc + pltpu.roll(acc, d, axis=0)
            d *= 2
        cnt = jnp.minimum(pos + 1, win).astype(f32)
        diff = acc[HALO:, :] / cnt - u_ext[HALO:, :]
        y = _dot(diff.astype(bf16), pw_ref[gi])
        pool_ref[0, :, c0:c0 + POOL_GROUP] = y * ps_ref[:, c0:c0 + POOL_GROUP]

    def shifted(c0, c1):
        cur = e_ref[HALO:HALO + tt, c0:c1]
        before = e_ref[HALO - 1:HALO - 1 + tt, c0:c1]
        return cur + (before - cur) * mu_ref[:, c0:c1]

    nxt = stash_ref.at[s % 2]
    nxt[0] = shifted(C_R, C_K)
    nxt[1] = shifted(C_K, C_V)
    v_new = shifted(C_V, C_LO1)
    lo1 = shifted(C_LO1, C_LO2)
    lo2 = shifted(C_LO2, N_COMB)
    zw = w0_ref[...] + _dot(jnp.tanh(lo1).astype(bf16), wup_ref[...])
    nxt[3] = (-jnp.exp(f32(-0.5))) * _sigmoid(zw)
    nxt[4] = _sigmoid(a0_ref[...] + _dot(lo1.astype(bf16), aup_ref[...]))
    g_ref[0] = _dot(_sigmoid(lo2).astype(bf16), gup_ref[...])
    if has_vres:
        v_mix = _sigmoid(v0_ref[...] + _dot(lo2.astype(bf16), vup_ref[...]))
        v_new = v_new + (vf_ref[0] - v_new) * v_mix
    else:
        vout_ref[0] = v_new
    nxt[2] = v_new

    e_ref[0:HALO, :] = jnp.where(s < n_tiles - 1, e_ref[tt:tt + HALO, :], e_ref[0:HALO, :])


def _mix_in(h, g1, w_comb, mu, pool_w, pool_scale, w0, wup, a0, aup, gup, k_k, k_a, r_k, vres):
    bsz, tp, _ = h.shape
    tt = TT_MIX
    tiles_per_row = tp // tt
    n_tiles = bsz * tiles_per_row
    egc_rows = tt // CHUNK * SUBLANES
    has_vres = vres is not None

    def cur_map(s):
        t = jnp.minimum(s, n_tiles - 1)
        return (t // tiles_per_row, t % tiles_per_row, 0)

    def lag_map(s):
        t = jnp.maximum(s - 1, 0)
        return (t // tiles_per_row, t % tiles_per_row, 0)

    cur = lambda width: pl.BlockSpec((1, tt, width), cur_map)
    lag = lambda width: pl.BlockSpec((1, tt, width), lag_map)
    cspec = lambda shape: pl.BlockSpec(shape, lambda s: (0,) * len(shape), pipeline_mode=pl.Buffered(1))
    in_specs = [
        cur(D_MODEL), cspec((1, D_MODEL)), cspec((D_MODEL, N_COMB)), cspec((1, N_COMB)),
        cspec((len(POOL_WINDOWS), POOL_GROUP, POOL_GROUP)),
        cspec((1, POOL_WIDTH)), cspec((1, RWKV_WIDTH)), cspec((LO1_W, RWKV_WIDTH)),
        cspec((1, RWKV_WIDTH)), cspec((LO1_W, RWKV_WIDTH)), cspec((LO2_W, RWKV_WIDTH)),
        cspec((1, RWKV_WIDTH)), cspec((1, RWKV_WIDTH)), cspec((1, RWKV_WIDTH)),
    ]
    args = [h, g1, w_comb, mu, pool_w, pool_scale, w0, wup, a0, aup, gup, k_k, k_a, r_k]
    wide = lambda dtype: jax.ShapeDtypeStruct((bsz, tp, RWKV_WIDTH), dtype)
    out_shape = [wide(f32)] + [wide(bf16)] * 7 + [wide(f32), wide(f32),
                 jax.ShapeDtypeStruct((bsz, tp // CHUNK * SUBLANES, RWKV_WIDTH), f32)]
    out_specs = [cur(RWKV_WIDTH)] + [lag(RWKV_WIDTH)] * 8 + [
        cur(RWKV_WIDTH), pl.BlockSpec((1, egc_rows, RWKV_WIDTH), lag_map)]
    if has_vres:
        v0, vup, v_first = vres
        in_specs += [cspec((1, RWKV_WIDTH)), cspec((LO2_W, RWKV_WIDTH)), cur(RWKV_WIDTH)]
        args += [v0, vup, v_first]
    else:
        out_shape.append(wide(f32))
        out_specs.append(cur(RWKV_WIDTH))
    return pl.pallas_call(
        functools.partial(_mix_in_kernel, has_vres=has_vres, tt=tt, n_tiles=n_tiles,
                          tiles_per_row=tiles_per_row),
        out_shape=tuple(out_shape),
        grid=(n_tiles + 1,),
        in_specs=in_specs,
        out_specs=tuple(out_specs),
        scratch_shapes=[pltpu.VMEM((HALO + tt, N_COMB), f32),
                        pltpu.VMEM((2, 5, tt, RWKV_WIDTH), f32)],
        compiler_params=pltpu.CompilerParams(
            dimension_semantics=("arbitrary",), vmem_limit_bytes=VMEM_LIMIT),
        name="mix_in_vres" if has_vres else "mix_in",
    )(*args)


def _scan_kernel(at_ref, rt_ref, bt_ref, kt_ref, bh_ref, kh_ref, egc_ref,
                 vb_ref, bonus_ref, g_ref, gnw_ref, gnb_ref,
                 o_ref, z_ref, wb_ref, zy_ref, dcol_ref, *, ts, tiles_per_row):
    s = pl.program_id(0)
    wr = s % 2
    rd = (s + 1) % 2

    @pl.when(s == 0)
    def _():
        wb_ref[...] = jnp.zeros(wb_ref.shape, bf16)
        zy_ref[...] = jnp.zeros(zy_ref.shape, bf16)
        dcol_ref[...] = jnp.zeros(dcol_ref.shape, f32)

    @pl.when((s + tiles_per_row - 1) % tiles_per_row == 0)
    def _():
        z_ref[...] = jnp.zeros(z_ref.shape, f32)

    c = CHUNK
    n_chunks = ts // c
    row2 = lax.broadcasted_iota(jnp.int32, (2 * c, 2 * c), 0)
    col2 = lax.broadcasted_iota(jnp.int32, (2 * c, 2 * c), 1)
    rm, cm = row2 & (c - 1), col2 & (c - 1)
    a_mask = (rm > cm) | ((row2 >= c) & (rm == cm))
    rowv = lax.broadcasted_iota(jnp.int32, (c, 2 * c), 0)
    colv = lax.broadcasted_iota(jnp.int32, (c, 2 * c), 1)
    low_half = colv < c
    heads = range(N_HEADS)
    hsl = _HEAD_SLICES

    def chunk_rows(ci):
        return pl.ds(pl.multiple_of(ci * c, c), c)

    def solve_products(cis):
        ws, xs = {}, {}
        pair_lo = lax.broadcasted_iota(jnp.int32, (2 * c, 2 * c), 1) < c
        zeros_cc = jnp.zeros((c, c), bf16)
        eye_hi = (colv == rowv + c).astype(f32)
        for n, ci in enumerate(cis):
            rows = chunk_rows(ci)
            at, rt, bt, kt = at_ref[0, rows, :], rt_ref[0, rows, :], bt_ref[0, rows, :], kt_ref[0, rows, :]
            bh, kh = bh_ref[0, rows, :], kh_ref[0, rows, :]
            egc = egc_ref[0, pl.ds(pl.multiple_of(ci * SUBLANES, SUBLANES), SUBLANES), :][0:1, :]
            for p in range(N_HEADS // 2):
                lanes = slice(2 * c * p, 2 * c * (p + 1))
                la = jnp.concatenate([at[:, lanes], rt[:, lanes]], axis=0)
                ra = jnp.concatenate([bt[:, lanes], kt[:, lanes]], axis=0)
                zero = jnp.zeros_like(la)
                am2 = _dot_nt(jnp.concatenate([jnp.where(pair_lo, la, zero),
                                               jnp.where(pair_lo, zero, la)], axis=0), ra)
                hat2 = jnp.transpose(jnp.concatenate([bh[:, lanes], kh[:, lanes]], axis=0))
                dcol = jnp.transpose(jnp.broadcast_to(egc[:, lanes], (c, 2 * c)))
                at_pair = at[:, lanes].astype(f32)
                at_low = [at_pair, pltpu.roll(at_pair, c, axis=1)]
                for q in range(2):
                    h = 2 * p + q
                    amat = jnp.where(a_mask, am2[2 * c * q:2 * c * (q + 1), :], 0.0)
                    top = amat[0:c, :]
                    ws[n, h] = jnp.where(low_half, top, eye_hi)
                    xs[n, h] = jnp.where(low_half, at_low[q], top).astype(bf16)
                    zy_ref[wr, ci, h] = jnp.concatenate(
                        [jnp.concatenate([hat2[c * q:c * (q + 1), :], zeros_cc], axis=1),
                         jnp.concatenate([amat[c:2 * c, :].astype(bf16), rt[:, hsl[h]]], axis=1)], axis=0)
                    dcol_ref[wr, ci, h] = dcol[c * q:c * (q + 1), :]
        return ws, xs

    def solve_level(ws, lvl):
        same = low_half & ((rowv >> (lvl + 1)) == (colv >> (lvl + 1)))
        for key in ws:
            w = ws[key]
            dmat = jnp.where(same, w, 0.0)[:, 0:c]
            ws[key] = jnp.where(same, 0.0, w) + _dot(dmat.astype(bf16), w.astype(bf16))

    def solve_store(ws, xs, cis):
        zeros_rows = jnp.zeros((c, 2 * c), bf16)
        for (n, h), w in ws.items():
            wb_ref[wr, cis[n], h] = _dot(
                w.astype(bf16), jnp.concatenate([zeros_rows, xs[n, h]], axis=0)).astype(bf16)

    def recur_u(ci):
        vb = vb_ref[0, chunk_rows(ci), :]
        zs = [z_ref[h] for h in heads]
        zbs = [z.astype(bf16) for z in zs]
        vbs = [vb[:, hsl[h]] for h in heads]
        ubs = [_dot(wb_ref[rd, ci, h], jnp.concatenate([zbs[h], vbs[h]], axis=0)).astype(bf16)
               for h in heads]
        return zs, zbs, vbs, ubs

    def recur_zy(ci, ctx):
        zs, zbs, vbs, ubs = ctx
        ys = []
        for h in heads:
            zy = _dot(zy_ref[rd, ci, h], jnp.concatenate([ubs[h], vbs[h], zbs[h]], axis=0))
            z_ref[h] = dcol_ref[rd, ci, h] * zs[h] + zy[0:c, :]
            ys.append(zy[c:2 * c, :])
        rows = chunk_rows(ci)
        ycs = [y - jnp.mean(y, axis=-1, keepdims=True) for y in ys]
        rstd = [lax.rsqrt(jnp.mean(yc * yc, axis=-1, keepdims=True) + GN_EPS) for yc in ycs]
        yn = jnp.concatenate(ycs, axis=1) * _head_bcast(rstd, low_half) * gnw_ref[...] + gnb_ref[...]
        o_ref[0, rows, :] = (yn + bonus_ref[0, rows, :]) * g_ref[0, rows, :]

    def group(cis):
        state = {}

        def products(n):
            state[n] = solve_products([cis[n]])

        def level(lvl, n):
            solve_level(state[n][0], lvl)

        def store(n):
            solve_store(state[n][0], state[n][1], [cis[n]])

        members = range(len(cis))
        solve = ([functools.partial(products, n) for n in members]
                 + [functools.partial(level, lvl, n) for lvl in range(6) for n in members]
                 + [functools.partial(store, n) for n in members])
        ctx = [None]

        def recur_first(ci):
            ctx[0] = recur_u(ci)

        def recur_second(ci):
            recur_zy(ci, ctx[0])

        recur = []
        for ci in cis:
            recur += [functools.partial(recur_first, ci), functools.partial(recur_second, ci)]
        assert len(recur) <= len(solve)
        for k, stage in enumerate(solve):
            if k < len(recur):
                recur[k]()
            stage()

    n_full = n_chunks // SCAN_GROUP

    def full_body(i, carry):
        group([SCAN_GROUP * i + k for k in range(SCAN_GROUP)])
        return carry

    lax.fori_loop(0, n_full, full_body, 0)
    if n_chunks % SCAN_GROUP:
        group(list(range(SCAN_GROUP * n_full, n_chunks)))


def _scan(at, rt, bt, kt, bh, kh, vb, bonus, g, egc, gn_w, gn_b):
    bsz, tp, _ = at.shape
    ts = TT_SCAN
    n_chunks = ts // CHUNK
    tiles_per_row = tp // ts
    n_tiles = bsz * tiles_per_row

    def cur_map(s):
        t = jnp.minimum(s, n_tiles - 1)
        return (t // tiles_per_row, t % tiles_per_row, 0)

    def lag_map(s):
        t = jnp.maximum(s - 1, 0)
        return (t // tiles_per_row, t % tiles_per_row, 0)

    cur = pl.BlockSpec((1, ts, RWKV_WIDTH), cur_map)
    lag = pl.BlockSpec((1, ts, RWKV_WIDTH), lag_map)
    ecur = pl.BlockSpec((1, n_chunks * SUBLANES, RWKV_WIDTH), cur_map)
    cspec = pl.BlockSpec((1, RWKV_WIDTH), lambda s: (0, 0))
    return pl.pallas_call(
        functools.partial(_scan_kernel, ts=ts, tiles_per_row=tiles_per_row),
        out_shape=jax.ShapeDtypeStruct((bsz, tp, RWKV_WIDTH), f32),
        grid=(n_tiles + 1,),
        in_specs=[cur] * 6 + [ecur] + [lag] * 3 + [cspec] * 2,
        out_specs=lag,
        scratch_shapes=[
            pltpu.VMEM((N_HEADS, HEAD_SIZE, HEAD_SIZE), f32),
            pltpu.VMEM((2, n_chunks, N_HEADS, CHUNK, 2 * CHUNK), bf16),
            pltpu.VMEM((2, n_chunks, N_HEADS, 2 * CHUNK, 3 * CHUNK), bf16),
            pltpu.VMEM((2, n_chunks, N_HEADS, HEAD_SIZE, HEAD_SIZE), f32),
        ],
        compiler_params=pltpu.CompilerParams(
            dimension_semantics=("arbitrary",), vmem_limit_bytes=VMEM_LIMIT),
        name="rwkv_scan",
    )(at, rt, bt, kt, bh, kh, egc, vb, bonus, g, gn_w, gn_b)


def _out_ffn_kernel(h_ref, pool_ref, rw_ref, wo_ref, g2_ref, wg_ref, wu_ref, wd_ref, gf_ref,
                    o_ref, *, final):
    load = (lambda ref: ref[0]) if final else (lambda ref: ref[...])
    mixed = (_dot(load(pool_ref).astype(bf16), wo_ref[0:POOL_WIDTH, :])
             + _dot(load(rw_ref).astype(bf16), wo_ref[POOL_WIDTH:, :]))
    h1 = load(h_ref) + mixed
    hn = h1 * lax.rsqrt(jnp.mean(h1 * h1, axis=-1, keepdims=True) + RMS_EPS) * g2_ref[...]
    hb = hn.astype(bf16)
    acc = h1
    start = 0
    for width in FF_BLKS:
        cs = slice(start, start + width)
        start += width
        gate = _dot(hb, wg_ref[:, cs])
        up = _dot(hb, wu_ref[:, cs])
        act = gate * _sigmoid(gate) * up
        acc = acc + _dot(act.astype(bf16), wd_ref[cs, :])
    if final:
        o_ref[0] = acc * lax.rsqrt(jnp.mean(acc * acc, axis=-1, keepdims=True) + RMS_EPS) * gf_ref[...]
    else:
        o_ref[...] = acc


def _out_ffn(h, pool, rw, w_out, g2, w_gate_up, w_down, g_final, layer, final, seq):
    bsz, tp, _ = h.shape
    tm = TM_FFN
    assert sum(FF_BLKS) == D_FF
    cspec = lambda shape: pl.BlockSpec(shape, lambda *_: (0, 0), pipeline_mode=pl.Buffered(1))
    stacked = lambda shape, half=0: pl.BlockSpec(
        (None,) + shape, lambda *_: (layer, 0, half), pipeline_mode=pl.Buffered(1))
    weights = [stacked((D_MODEL, D_MODEL)), cspec((1, D_MODEL)),
               stacked((D_MODEL, D_FF), 0), stacked((D_MODEL, D_FF), 1),
               stacked((D_FF, D_MODEL)), cspec((1, D_MODEL))]
    if final:
        assert seq % tm == 0 and N_META % SUBLANES == 0
        win = lambda width: pl.BlockSpec(
            (pl.Element(1), pl.Element(tm), pl.Element(width)),
            lambda b, i: (b, pl.multiple_of(N_META + i * tm, SUBLANES), 0))
        args = (h, pool, rw)
        grid = (bsz, seq // tm)
        out_shape = jax.ShapeDtypeStruct((bsz, seq, D_MODEL), f32)
        out_spec = pl.BlockSpec((1, tm, D_MODEL), lambda b, i: (b, i, 0))
        sem = ("arbitrary", "arbitrary")
    else:
        rows = bsz * tp
        win = lambda width: pl.BlockSpec((tm, width), lambda i: (i, 0))
        args = tuple(t.reshape(rows, t.shape[-1]) for t in (h, pool, rw))
        grid = (rows // tm,)
        out_shape = jax.ShapeDtypeStruct((rows, D_MODEL), f32)
        out_spec = win(D_MODEL)
        sem = ("arbitrary",)
    out = pl.pallas_call(
        functools.partial(_out_ffn_kernel, final=final),
        out_shape=out_shape,
        grid=grid,
        in_specs=[win(D_MODEL), win(POOL_WIDTH), win(RWKV_WIDTH)] + weights,
        out_specs=out_spec,
        compiler_params=pltpu.CompilerParams(dimension_semantics=sem, vmem_limit_bytes=VMEM_LIMIT),
        name="out_ffn_final" if final else "out_ffn",
    )(*args, w_out, g2, w_gate_up, w_gate_up, w_down, g_final)
    return out if final else out.reshape(bsz, tp, D_MODEL)


def _row(p):
    return p.reshape(1, -1).astype(f32)


def _pad_rows(w, before, total):
    return jnp.pad(w, ((before, total - before - w.shape[0]), (0, 0)))


def kernel(x, meta_tokens, ln1_g, w_in, mu_shift, pool_w, pool_scale, w0, w_lora_up, a0, a_lora_up, g_lora_up, k_k, k_a, r_k, gn_w, gn_b, w_in_vres, mu_vres, v0, v_lora_up, w_out, ln2_g, w_gate_up, w_down, final_g):
    bsz, seq, _ = x.shape
    depth = w_in.shape[0]
    t_real = N_META + seq
    tp = SEQ_PAD_TO
    assert t_real <= tp and tp % TT_MIX == 0 and tp % TT_SCAN == 0 and (bsz * tp) % TM_FFN == 0
    meta = jnp.broadcast_to(meta_tokens.astype(x.dtype)[None], (bsz, N_META, D_MODEL))
    h = jnp.concatenate([meta, x, jnp.zeros((bsz, tp - t_real, D_MODEL), x.dtype)], axis=1)

    w_out_b, w_gate_up_b, w_down_b = (w.astype(bf16) for w in (w_out, w_gate_up, w_down))
    v_first = None
    for l in range(depth):
        n_in = w_in.shape[2]
        w_cols = [w_in[l]]
        mu_cols = [jnp.zeros((POOL_WIDTH,), f32), mu_shift[l].astype(f32)]
        used = n_in
        if l > 0:
            w_cols.append(w_in_vres[l - 1])
            mu_cols.append(mu_vres[l - 1].astype(f32))
            used += MV_LORA
        w_comb = jnp.pad(jnp.concatenate(w_cols, axis=1), ((0, 0), (0, N_COMB - used))).astype(bf16)
        mu = jnp.pad(jnp.concatenate(mu_cols), (0, N_COMB - used)).reshape(1, N_COMB)
        wup = _pad_rows(w_lora_up[l], 0, LO1_W).astype(bf16)
        aup = _pad_rows(a_lora_up[l], DECAY_LORA, LO1_W).astype(bf16)
        gup = _pad_rows(g_lora_up[l], 0, LO2_W).astype(bf16)
        vres = None
        if l > 0:
            vup = _pad_rows(v_lora_up[l - 1], GATE_LORA, LO2_W).astype(bf16)
            vres = (_row(v0[l - 1]), vup, v_first)
        outs = _mix_in(
            h, _row(ln1_g[l]), w_comb, mu, pool_w[l].astype(bf16), _row(pool_scale[l]),
            _row(w0[l]), wup, _row(a0[l]), aup, gup, _row(k_k[l]), _row(k_a[l]), _row(r_k[l]), vres)
        pool_o, scan_args = outs[0], outs[1:11]
        if l == 0:
            v_first = outs[11]
        rw = _scan(*scan_args, _row(gn_w[l]), _row(gn_b[l]))
        h = _out_ffn(h, pool_o, rw, w_out_b, _row(ln2_g[l]), w_gate_up_b, w_down_b, _row(final_g),
                     layer=l, final=(l == depth - 1), seq=seq)
    return h
```
